```python
import jax, jax.numpy as jnp
from jax import lax
import numpy as np

D_MODEL = 4096
BATCH = 1
SEQ = 16384
DEPTH = 2

N_A_LAYERS = DEPTH // 2
N_B_LAYERS = DEPTH - N_A_LAYERS
POOL_WINDOWS = (2, 4, 8, 16)
N_POOL_GROUPS = len(POOL_WINDOWS)
POOL_GROUP = D_MODEL // N_POOL_GROUPS
HEAD_DIM = 128
N_HEADS = D_MODEL // HEAD_DIM
D_FF = ((8 * D_MODEL // 3 + 255) // 256) * 256
CONV_WIDTH = 3
PLE_DIM = 256
Q_BLOCK = 128
EPS = 1e-6

kernel_name = "yoco_pool_fox_convffn_hybrid"


def rmsnorm(x, g):
    xf = x.astype(jnp.float32)
    y = xf * lax.rsqrt(jnp.mean(xf * xf, axis=-1, keepdims=True) + EPS)
    return (y * g.astype(jnp.float32)).astype(x.dtype)


def multiscale_pool_mixer(h, w_pool, pool_scale):
    B, S, D = h.shape
    hg = h.reshape(B, S, N_POOL_GROUPS, POOL_GROUP)
    pos = jnp.arange(S)
    outs = []
    for g, w in enumerate(POOL_WINDOWS):
        xg = hg[:, :, g, :].astype(jnp.float32)
        cs = jnp.cumsum(xg, axis=1)
        lagged = jnp.pad(cs, ((0, 0), (w, 0), (0, 0)))[:, :S]
        cnt = jnp.minimum(pos + 1, w).astype(jnp.float32)[None, :, None]
        outs.append((cs - lagged) / cnt - xg)
    pooled = jnp.stack(outs, axis=2).astype(h.dtype)
    mixed = jnp.einsum('bsgc,gcd->bsgd', pooled, w_pool)
    return mixed.reshape(B, S, D) * pool_scale


def conv_ffn(h, w_in, conv_w, conv_b, w_out):
    u = h @ w_in
    u = lax.conv_general_dilated(
        u, conv_w[:, None, :].astype(u.dtype), window_strides=(1,),
        padding=[(CONV_WIDTH - 1, 0)],
        dimension_numbers=('NWC', 'WIO', 'NWC'),
        feature_group_count=u.shape[-1]) + conv_b
    gate, val = jnp.split(u, 2, axis=-1)
    return (jax.nn.silu(gate) * val) @ w_out


def per_layer_embed(x, p_i, w_ple, gate_norm, w_gate):
    gate = jax.nn.sigmoid(rmsnorm(x, gate_norm) @ w_gate)
    return x + gate * (p_i @ w_ple)


def shared_kv(s, kv_norm, w_kvf, b_f, k_norm):
    B, S, D = s.shape
    kvf = rmsnorm(s, kv_norm) @ w_kvf
    k = rmsnorm(kvf[..., :D].reshape(B, S, N_HEADS, HEAD_DIM), k_norm)
    v = kvf[..., D:2 * D].reshape(B, S, N_HEADS, HEAD_DIM)
    log_f = jax.nn.log_sigmoid((kvf[..., 2 * D:] + b_f).astype(jnp.float32))
    c = jnp.cumsum(log_f, axis=1)
    return k, v, c


def forgetting_attention(q, k, v, c):
    B, S, H, hd = q.shape
    nb = S // Q_BLOCK
    qb = q.reshape(B, nb, Q_BLOCK, H, hd).transpose(1, 0, 2, 3, 4)
    cb = c.reshape(B, nb, Q_BLOCK, H).transpose(1, 0, 3, 2)
    c_keys = c.transpose(0, 2, 1)
    key_pos = jnp.arange(S)
    scale = hd ** -0.5

    def block(args):
        i, q_i, c_i = args
        logits = jnp.einsum('bqhd,bkhd->bhqk', q_i, k,
                            preferred_element_type=jnp.float32) * scale
        logits = logits + c_i[..., :, None] - c_keys[:, :, None, :]
        q_pos = i * Q_BLOCK + jnp.arange(Q_BLOCK)
        mask = key_pos[None, :] <= q_pos[:, None]
        probs = jax.nn.softmax(jnp.where(mask, logits, -jnp.inf), axis=-1)
        return jnp.einsum('bhqk,bkhd->bqhd', probs.astype(v.dtype), v)

    out = lax.map(block, (jnp.arange(nb), qb, cb))
    return out.transpose(1, 0, 2, 3, 4).reshape(B, S, H * hd)


def setup_inputs(seed: int = 0) -> dict:
    key = jax.random.key(seed)
    ks = jax.random.split(key, 24)
    D, F, H = D_MODEL, D_FF, N_HEADS

    def nrm(k, shape, s):
        return jax.random.normal(k, shape, jnp.float32) * s

    def gain(k, shape):
        return 1.0 + 0.02 * jax.random.normal(k, shape, jnp.float32)

    w_kvf = jnp.concatenate([nrm(ks[10], (D, 2 * D), D ** -0.5),
                             nrm(ks[11], (D, H), 0.3 * D ** -0.5)], axis=1)
    b_f = jnp.linspace(2.0, 7.0, H, dtype=jnp.float32) + nrm(ks[12], (H,), 0.1)
    return {
        "x": nrm(ks[0], (BATCH, SEQ, D), 1.0),
        "p": nrm(ks[1], (DEPTH, BATCH, SEQ, PLE_DIM), 1.0),
        "pool_norm": gain(ks[2], (N_A_LAYERS, D)),
        "w_pool": nrm(ks[3], (N_A_LAYERS, N_POOL_GROUPS, POOL_GROUP, POOL_GROUP), POOL_GROUP ** -0.5),
        "pool_scale": gain(ks[4], (N_A_LAYERS, D)),
        "attn_norm": gain(ks[5], (N_B_LAYERS, D)),
        "w_q": nrm(ks[6], (N_B_LAYERS, D, D), D ** -0.5),
        "q_norm": gain(ks[7], (N_B_LAYERS, HEAD_DIM)),
        "w_o": nrm(ks[8], (N_B_LAYERS, D, D), D ** -0.5),
        "kv_norm": gain(ks[9], (D,)),
        "w_kvf": w_kvf,
        "b_f": b_f,
        "k_norm": gain(ks[13], (HEAD_DIM,)),
        "ffn_norm": gain(ks[14], (DEPTH, D)),
        "w_in": nrm(ks[15], (DEPTH, D, 2 * F), D ** -0.5),
        "conv_w": nrm(ks[16], (DEPTH, CONV_WIDTH, 2 * F), CONV_WIDTH ** -0.5),
        "conv_b": nrm(ks[17], (DEPTH, 2 * F), 0.01),
        "w_out": nrm(ks[18], (DEPTH, F, D), F ** -0.5),
        "w_ple": nrm(ks[19], (DEPTH, PLE_DIM, D), PLE_DIM ** -0.5),
        "gate_norm": gain(ks[20], (DEPTH, D)),
        "w_gate": nrm(ks[21], (DEPTH, D, D), D ** -0.5),
    }


def reference(x, p, pool_norm, w_pool, pool_scale, attn_norm, w_q, q_norm, w_o,
              kv_norm, w_kvf, b_f, k_norm, ffn_norm, w_in, conv_w, conv_b, w_out,
              w_ple, gate_norm, w_gate):
    B, S, D = x.shape
    k = v = c = None
    for i in range(DEPTH):
        if i < N_A_LAYERS:
            x = x + multiscale_pool_mixer(rmsnorm(x, pool_norm[i]), w_pool[i], pool_scale[i])
        else:
            j = i - N_A_LAYERS
            h = rmsnorm(x, attn_norm[j])
            q = rmsnorm((h @ w_q[j]).reshape(B, S, N_HEADS, HEAD_DIM), q_norm[j])
            x = x + forgetting_attention(q, k, v, c) @ w_o[j]
        x = x + conv_ffn(rmsnorm(x, ffn_norm[i]), w_in[i], conv_w[i], conv_b[i], w_out[i])
        x = per_layer_embed(x, p[i], w_ple[i], gate_norm[i], w_gate[i])
        if i == N_A_LAYERS - 1:
            k, v, c = shared_kv(x, kv_norm, w_kvf, b_f, k_norm)
    return x
```

```python
import functools
import math

import jax
import jax.numpy as jnp
from jax import lax
from jax.experimental import pallas as pl
from jax.experimental.pallas import tpu as pltpu

HEAD_DIM = 128
POOL_WINDOWS = (2, 4, 8, 16)
POOL_HALO = 16
CONV_WIDTH = 3
CARRY_ROWS = 8
EPS = 1e-6
LANE = 128
FF_ALIGN = 1024
VMEM_LIMIT = 56 * 1024 * 1024

F32 = jnp.float32
BF16 = jnp.bfloat16


def _tile(n, want):
    t = min(n, want)
    while n % t:
        t //= 2
    return t


def _params(*sem):
    return pltpu.CompilerParams(dimension_semantics=sem, vmem_limit_bytes=VMEM_LIMIT)


def _rms(x, g):
    ms = jnp.mean(x * x, axis=-1, keepdims=True)
    return x * lax.rsqrt(ms + EPS) * g


def _pool_kernel(x_ref, halo_ref, g_ref, w_ref, sc_ref, g2_ref, xo_ref, ho_ref, *, tm):
    i = pl.program_id(0)
    x = x_ref[...]
    g = g_ref[...]
    h = _rms(x, g)
    hh = jnp.where(i > 0, _rms(halo_ref[...], g), 0.0)
    cur = jnp.concatenate([hh, h], axis=0)
    d = x.shape[1]
    c = d // len(POOL_WINDOWS)
    pos = i * tm + lax.broadcasted_iota(jnp.int32, (tm, 1), 0)
    outs = []
    for gi, w in enumerate(POOL_WINDOWS):
        cur = cur + pltpu.roll(cur, w // 2, axis=0)
        win = cur[POOL_HALO:, :c]
        if gi + 1 < len(POOL_WINDOWS):
            cur = cur[:, c:]
        inv_cnt = 1.0 / jnp.minimum(pos + 1, w).astype(F32)
        pooled = win * inv_cnt - h[:, gi * c:(gi + 1) * c]
        outs.append(jnp.dot(pooled.astype(BF16), w_ref[gi], preferred_element_type=F32))
    x1 = x + jnp.concatenate(outs, axis=1) * sc_ref[...]
    xo_ref[...] = x1
    ho_ref[...] = _rms(x1, g2_ref[...]).astype(BF16)


def _pool_layer(x, g, w_pool, scale, g_next):
    s, d = x.shape
    tm = _tile(s, 256)
    hb = tm // POOL_HALO
    row = lambda i: (i, 0)
    const2 = lambda i: (0, 0)
    return pl.pallas_call(
        functools.partial(_pool_kernel, tm=tm),
        grid=(s // tm,),
        in_specs=[
            pl.BlockSpec((tm, d), row),
            pl.BlockSpec((POOL_HALO, d), lambda i: (jnp.maximum(i * hb - 1, 0), 0)),
            pl.BlockSpec((1, d), const2),
            pl.BlockSpec(w_pool.shape, lambda i: (0, 0, 0)),
            pl.BlockSpec((1, d), const2),
            pl.BlockSpec((1, d), const2),
        ],
        out_specs=[pl.BlockSpec((tm, d), row), pl.BlockSpec((tm, d), row)],
        out_shape=[jax.ShapeDtypeStruct((s, d), F32), jax.ShapeDtypeStruct((s, d), BF16)],
        compiler_params=_params("arbitrary"),
        name="pool_mixer",
    )(x, x, g, w_pool, scale, g_next)


def _norm_kernel(x_ref, *refs):
    n = len(refs) // 2
    x = x_ref[...]
    xn = x * lax.rsqrt(jnp.mean(x * x, axis=-1, keepdims=True) + EPS)
    for g_ref, o_ref in zip(refs[:n], refs[n:]):
        o_ref[...] = (xn * g_ref[...]).astype(BF16)


def _norm_cast(x, gains):
    s, d = x.shape
    tm = _tile(s, 512)
    row = lambda i: (i, 0)
    outs = pl.pallas_call(
        _norm_kernel,
        grid=(s // tm,),
        in_specs=[pl.BlockSpec((tm, d), row)] + [pl.BlockSpec((1, d), lambda i: (0, 0))] * len(gains),
        out_specs=[pl.BlockSpec((tm, d), row)] * len(gains),
        out_shape=[jax.ShapeDtypeStruct((s, d), BF16)] * len(gains),
        compiler_params=_params("arbitrary"),
        name="rmsnorm_cast",
    )(x, *gains)
    return outs


def _causal_conv(u, prev, cw, b):
    tm = u.shape[0]
    ext = jnp.concatenate([prev, u], axis=0)
    u1 = pltpu.roll(ext, 1, axis=0)[CARRY_ROWS:]
    u2 = pltpu.roll(ext, 2, axis=0)[CARRY_ROWS:]
    return cw[2:3] * u + cw[1:2] * u1 + cw[0:1] * u2 + b, ext[tm:]


def _ffn_in_kernel(a_ref, wg_ref, wv_ref, cwg_ref, cwv_ref, bg_ref, bv_ref, o_ref, carry_ref):
    i = pl.program_id(0)
    j = pl.program_id(1)

    @pl.when(i == 0)
    def _():
        carry_ref[j] = jnp.zeros(carry_ref.shape[1:], F32)

    a = a_ref[...]
    ug = jnp.dot(a, wg_ref[...], preferred_element_type=F32)
    uv = jnp.dot(a, wv_ref[...], preferred_element_type=F32)
    yg, tail_g = _causal_conv(ug, carry_ref[j, 0], cwg_ref[...], bg_ref[...])
    yv, tail_v = _causal_conv(uv, carry_ref[j, 1], cwv_ref[...], bv_ref[...])
    carry_ref[j, 0] = tail_g
    carry_ref[j, 1] = tail_v
    o_ref[...] = (yg * jax.nn.sigmoid(yg) * yv).astype(BF16)


def _ffn_in(h, w_in, conv_w, conv_b):
    s, d = h.shape
    fp = w_in.shape[1] // 2
    tm = _tile(s, 1024)
    tn = _tile(fp, 512)
    nj = fp // tn
    gate = lambda i, j: (0, j)
    val = lambda i, j: (0, j + nj)
    return pl.pallas_call(
        _ffn_in_kernel,
        grid=(s // tm, nj),
        in_specs=[
            pl.BlockSpec((tm, d), lambda i, j: (i, 0)),
            pl.BlockSpec((d, tn), gate),
            pl.BlockSpec((d, tn), val),
            pl.BlockSpec((CONV_WIDTH, tn), gate),
            pl.BlockSpec((CONV_WIDTH, tn), val),
            pl.BlockSpec((1, tn), gate),
            pl.BlockSpec((1, tn), val),
        ],
        out_specs=pl.BlockSpec((tm, tn), lambda i, j: (i, j)),
        out_shape=jax.ShapeDtypeStruct((s, fp), BF16),
        scratch_shapes=[pltpu.VMEM((nj, 2, CARRY_ROWS, tn), F32)],
        compiler_params=_params("arbitrary", "arbitrary"),
        name="ffn_in_conv_act",
    )(h, w_in, w_in, conv_w, conv_w, conv_b, conv_b)


def _mm_res_kernel(a_ref, w_ref, r_ref, o_ref, acc_ref, *, nk):
    k = pl.program_id(2)
    prod = jnp.dot(a_ref[...], w_ref[...], preferred_element_type=F32)
    if nk == 1:
        o_ref[...] = r_ref[...] + prod
        return

    @pl.when(k == 0)
    def _():
        acc_ref[...] = prod

    @pl.when(jnp.logical_and(k > 0, k < nk - 1))
    def _():
        acc_ref[...] += prod

    @pl.when(k == nk - 1)
    def _():
        o_ref[...] = r_ref[...] + (acc_ref[...] + prod)


def _mm_res(a, w, res, tk_want):
    s, kdim = a.shape
    n = w.shape[1]
    tm = _tile(s, 1024)
    tn = _tile(n, 1024)
    tk = _tile(kdim, tk_want)
    nk = kdim // tk
    return pl.pallas_call(
        functools.partial(_mm_res_kernel, nk=nk),
        grid=(s // tm, n // tn, nk),
        in_specs=[
            pl.BlockSpec((tm, tk), lambda i, j, k: (i, k)),
            pl.BlockSpec((tk, tn), lambda i, j, k: (k, j)),
            pl.BlockSpec((tm, tn), lambda i, j, k: (i, j)),
        ],
        out_specs=pl.BlockSpec((tm, tn), lambda i, j, k: (i, j)),
        out_shape=jax.ShapeDtypeStruct((s, n), F32),
        scratch_shapes=[pltpu.VMEM((tm, tn), F32)],
        compiler_params=_params("arbitrary", "arbitrary", "arbitrary"),
        name="matmul_residual",
    )(a, w, res)


def _ple_kernel(h_ref, wg_ref, p_ref, wp_ref, x_ref, o_ref):
    z = jnp.dot(h_ref[...], wg_ref[...], preferred_element_type=F32)
    e = jnp.dot(p_ref[...].astype(BF16), wp_ref[...], preferred_element_type=F32)
    o_ref[...] = x_ref[...] + jax.nn.sigmoid(z) * e


def _ple(hn, w_gate, p, w_ple, x):
    s, d = x.shape
    pd = p.shape[1]
    tm = _tile(s, 1024)
    tn = _tile(d, 512)
    return pl.pallas_call(
        _ple_kernel,
        grid=(s // tm, d // tn),
        in_specs=[
            pl.BlockSpec((tm, d), lambda i, j: (i, 0)),
            pl.BlockSpec((d, tn), lambda i, j: (0, j)),
            pl.BlockSpec((tm, pd), lambda i, j: (i, 0)),
            pl.BlockSpec((pd, tn), lambda i, j: (0, j)),
            pl.BlockSpec((tm, tn), lambda i, j: (i, j)),
        ],
        out_specs=pl.BlockSpec((tm, tn), lambda i, j: (i, j)),
        out_shape=jax.ShapeDtypeStruct((s, d), F32),
        compiler_params=_params("arbitrary", "arbitrary"),
        name="per_layer_embed",
    )(hn, w_gate, p, w_ple, x)


def _proj_kernel(a_ref, w_ref, g_ref, o_ref, *, n_norm_tiles, out_scale):
    j = pl.program_id(1)
    z = jnp.dot(a_ref[...], w_ref[...], preferred_element_type=F32)
    tn = z.shape[1]

    @pl.when(j < n_norm_tiles)
    def _():
        g = g_ref[...] * out_scale
        for hh in range(tn // HEAD_DIM):
            sl = slice(hh * HEAD_DIM, (hh + 1) * HEAD_DIM)
            zh = z[:, sl]
            ms = jnp.mean(zh * zh, axis=-1, keepdims=True)
            o_ref[:, sl] = (zh * lax.rsqrt(ms + EPS) * g).astype(BF16)

    @pl.when(j >= n_norm_tiles)
    def _():
        o_ref[...] = z.astype(BF16)


def _proj_heads(a, w, head_gain, n_norm_cols, out_scale):
    s, d = a.shape
    n = w.shape[1]
    tm = _tile(s, 1024)
    tn = _tile(n, 512)
    return pl.pallas_call(
        functools.partial(_proj_kernel, n_norm_tiles=n_norm_cols // tn, out_scale=out_scale),
        grid=(s // tm, n // tn),
        in_specs=[
            pl.BlockSpec((tm, d), lambda i, j: (i, 0)),
            pl.BlockSpec((d, tn), lambda i, j: (0, j)),
            pl.BlockSpec((1, HEAD_DIM), lambda i, j: (0, 0)),
        ],
        out_specs=pl.BlockSpec((tm, tn), lambda i, j: (i, j)),
        out_shape=jax.ShapeDtypeStruct((s, n), BF16),
        compiler_params=_params("arbitrary", "arbitrary"),
        name="head_projection",
    )(a, w, head_gain)


def _gate_kernel(a_ref, w_ref, b_ref, o_ref, carry_ref, *, tm):
    i = pl.program_id(0)

    @pl.when(i == 0)
    def _():
        carry_ref[...] = jnp.zeros(carry_ref.shape, F32)

    z = jnp.dot(a_ref[...], w_ref[...], preferred_element_type=F32) + b_ref[...]
    run = jnp.minimum(z, 0.0) - jnp.log1p(jnp.exp(-jnp.abs(z)))
    row = lax.broadcasted_iota(jnp.int32, run.shape, 0)
    shift = 1
    while shift < tm:
        run = run + jnp.where(row >= shift, pltpu.roll(run, shift, axis=0), 0.0)
        shift *= 2
    run = run + carry_ref[...]
    o_ref[...] = run
    carry_ref[...] = run[tm - 1:tm, :]


def _forget_cumsum(a, w_f, b_f):
    s, d = a.shape
    n = w_f.shape[1]
    tm = _tile(s, 512)
    return pl.pallas_call(
        functools.partial(_gate_kernel, tm=tm),
        grid=(s // tm,),
        in_specs=[
            pl.BlockSpec((tm, d), lambda i: (i, 0)),
            pl.BlockSpec((d, n), lambda i: (0, 0)),
            pl.BlockSpec((1, n), lambda i: (0, 0)),
        ],
        out_specs=pl.BlockSpec((tm, n), lambda i: (i, 0)),
        out_shape=jax.ShapeDtypeStruct((s, n), F32),
        scratch_shapes=[pltpu.VMEM((1, n), F32)],
        compiler_params=_params("arbitrary"),
        name="forget_gate_cumsum",
    )(a, w_f, b_f)


def _attn_kernel(q_ref, k_ref, v_ref, ck_ref, cq_ref, o_ref, m_ref, l_ref, acc_ref, *, tq, tk):
    h = pl.program_id(0)
    i = pl.program_id(1)
    q = q_ref[...]
    lane = lax.broadcasted_iota(jnp.int32, cq_ref.shape, 1)
    ct = jnp.sum(jnp.where(lane == h, cq_ref[...], 0.0), axis=1, keepdims=True)

    m_ref[...] = jnp.full(m_ref.shape, -jnp.inf, F32)
    l_ref[...] = jnp.zeros(l_ref.shape, F32)
    acc_ref[...] = jnp.zeros(acc_ref.shape, F32)

    def step(start, masked):
        k = k_ref[pl.ds(start, tk), :]
        v = v_ref[pl.ds(start, tk), :]
        cs = ck_ref[0, :, pl.ds(start, tk)]
        s = lax.dot_general(q, k, (((1,), (1,)), ((), ())), preferred_element_type=F32) - cs
        if masked:
            qpos = i * tq + lax.broadcasted_iota(jnp.int32, (tq, tk), 0)
            kpos = start + lax.broadcasted_iota(jnp.int32, (tq, tk), 1)
            s = jnp.where(kpos <= qpos, s, -jnp.inf)
        m_old = m_ref[...]
        m_new = jnp.maximum(m_old, jnp.max(s, axis=1, keepdims=True) + ct)
        p = jnp.exp(s - (m_new - ct))
        alpha = jnp.exp(m_old - m_new)
        l_ref[...] = alpha * l_ref[...] + jnp.sum(p, axis=1, keepdims=True)
        acc_ref[...] = alpha * acc_ref[...] + jnp.dot(p.astype(BF16), v, preferred_element_type=F32)
        m_ref[...] = m_new

    n_full = (i * tq) // tk

    def body(j, carry):
        step(pl.multiple_of(j * tk, tk), masked=False)
        return carry

    lax.fori_loop(0, n_full, body, 0)
    for jj in range(tq // tk):
        step(pl.multiple_of(i * tq + jj * tk, tk), masked=True)
    o_ref[...] = (acc_ref[...] / l_ref[...]).astype(BF16)


def _attention(q, kv, c, c_keys):
    s, d = q.shape
    nh = d // HEAD_DIM
    tq = _tile(s, 1024)
    tk = _tile(tq, 512)
    return pl.pallas_call(
        functools.partial(_attn_kernel, tq=tq, tk=tk),
        grid=(nh, s // tq),
        in_specs=[
            pl.BlockSpec((tq, HEAD_DIM), lambda h, i: (i, h)),
            pl.BlockSpec((s, HEAD_DIM), lambda h, i: (0, h)),
            pl.BlockSpec((s, HEAD_DIM), lambda h, i: (0, nh + h)),
            pl.BlockSpec((1, 1, s), lambda h, i: (h, 0, 0)),
            pl.BlockSpec((tq, c.shape[1]), lambda h, i: (i, 0)),
        ],
        out_specs=pl.BlockSpec((tq, HEAD_DIM), lambda h, i: (i, h)),
        out_shape=jax.ShapeDtypeStruct((s, d), BF16),
        scratch_shapes=[
            pltpu.VMEM((tq, 1), F32),
            pltpu.VMEM((tq, 1), F32),
            pltpu.VMEM((tq, HEAD_DIM), F32),
        ],
        compiler_params=_params("arbitrary", "arbitrary"),
        name="forgetting_attention",
    )(q, kv, kv, c_keys, c)


def _pad_ff(t, f, fp, axis):
    pad = [(0, 0)] * t.ndim
    pad[axis] = (0, fp - f)
    return jnp.pad(t, pad)


def _split_pad(t, f, fp):
    return jnp.concatenate([_pad_ff(t[..., :f], f, fp, -1), _pad_ff(t[..., f:], f, fp, -1)], axis=-1)


def _conv_ffn(x, hn, w_in, conv_w, conv_b, w_out):
    f = w_out.shape[0]
    fp = -(-f // FF_ALIGN) * FF_ALIGN
    act = _ffn_in(hn, _split_pad(w_in, f, fp).astype(BF16), _split_pad(conv_w, f, fp),
                  _split_pad(conv_b[None, :], f, fp))
    return _mm_res(act, _pad_ff(w_out, f, fp, 0).astype(BF16), x, fp // 4)


def kernel(x, p, pool_norm, w_pool, pool_scale, attn_norm, w_q, q_norm, w_o, kv_norm, w_kvf, b_f,
           k_norm, ffn_norm, w_in, conv_w, conv_b, w_out, w_ple, gate_norm, w_gate):
    b, s, d = x.shape
    assert b == 1 and d % (HEAD_DIM * len(POOL_WINDOWS)) == 0
    depth = p.shape[0]
    n_a = pool_norm.shape[0]
    nh = d // HEAD_DIM
    x = x[0]
    row = lambda t: t[None, :]
    kv = c = c_keys = None
    for i in range(depth):
        if i < n_a:
            x, hn = _pool_layer(x, row(pool_norm[i]), w_pool[i].astype(BF16), row(pool_scale[i]),
                                row(ffn_norm[i]))
        else:
            j = i - n_a
            (hq,) = _norm_cast(x, [row(attn_norm[j])])
            q = _proj_heads(hq, w_q[j].astype(BF16), row(q_norm[j]), d, HEAD_DIM ** -0.5)
            att = _attention(q, kv, c, c_keys)
            x = _mm_res(att, w_o[j].astype(BF16), x, d)
            (hn,) = _norm_cast(x, [row(ffn_norm[i])])
        x = _conv_ffn(x, hn, w_in[i], conv_w[i], conv_b[i], w_out[i])
        (hg,) = _norm_cast(x, [row(gate_norm[i])])
        x = _ple(hg, w_gate[i].astype(BF16), p[i, 0], w_ple[i].astype(BF16), x)
        if i == n_a - 1:
            (hk,) = _norm_cast(x, [row(kv_norm)])
            kv = _proj_heads(hk, w_kvf[:, :2 * d].astype(BF16), row(k_norm), d, 1.0)
            w_f = jnp.pad(w_kvf[:, 2 * d:], ((0, 0), (0, LANE - nh))).astype(BF16)
            c = _forget_cumsum(hk, w_f, jnp.pad(b_f, (0, LANE - nh))[None, :])
            c_keys = c[:, :nh].T[:, None, :]
    return x[None]
```

```python
import functools
import math

import jax
import jax.numpy as jnp
from jax import lax
from jax.experimental import pallas as pl
from jax.experimental.pallas import tpu as pltpu

HEAD_DIM = 128
POOL_WINDOWS = (2, 4, 8, 16)
POOL_HALO = 16
CONV_WIDTH = 3
CARRY_ROWS = 8
EPS = 1e-6
LOG2E = math.log2(math.e)
LANE = 128
FF_ALIGN = 1024
VMEM_LIMIT = 56 * 1024 * 1024

F32 = jnp.float32
BF16 = jnp.bfloat16


def _tile(n, want):
    t = min(n, want)
    while n % t:
        t //= 2
    return t


def _params(*sem):
    return pltpu.CompilerParams(dimension_semantics=sem, vmem_limit_bytes=VMEM_LIMIT)


def _rms(x, g):
    ms = jnp.mean(x * x, axis=-1, keepdims=True)
    return x * lax.rsqrt(ms + EPS) * g


def _pool_kernel(x_ref, halo_ref, g_ref, w_ref, sc_ref, g2_ref, xo_ref, ho_ref, *, tm):
    i = pl.program_id(0)
    x = x_ref[...]
    g = g_ref[...]
    h = _rms(x, g)
    hh = jnp.where(i > 0, _rms(halo_ref[...], g), 0.0)
    cur = jnp.concatenate([hh, h], axis=0)
    d = x.shape[1]
    c = d // len(POOL_WINDOWS)
    pos = i * tm + lax.broadcasted_iota(jnp.int32, (tm, 1), 0)
    outs = []
    for gi, w in enumerate(POOL_WINDOWS):
        cur = cur + pltpu.roll(cur, w // 2, axis=0)
        win = cur[POOL_HALO:, :c]
        if gi + 1 < len(POOL_WINDOWS):
            cur = cur[:, c:]
        inv_cnt = 1.0 / jnp.minimum(pos + 1, w).astype(F32)
        pooled = win * inv_cnt - h[:, gi * c:(gi + 1) * c]
        outs.append(jnp.dot(pooled.astype(BF16), w_ref[gi], preferred_element_type=F32))
    x1 = x + jnp.concatenate(outs, axis=1) * sc_ref[...]
    xo_ref[...] = x1
    ho_ref[...] = _rms(x1, g2_ref[...]).astype(BF16)


def _pool_layer(x, g, w_pool, scale, g_next):
    s, d = x.shape
    tm = _tile(s, 256)
    hb = tm // POOL_HALO
    row = lambda i: (i, 0)
    const2 = lambda i: (0, 0)
    return pl.pallas_call(
        functools.partial(_pool_kernel, tm=tm),
        grid=(s // tm,),
        in_specs=[
            pl.BlockSpec((tm, d), row),
            pl.BlockSpec((POOL_HALO, d), lambda i: (jnp.maximum(i * hb - 1, 0), 0)),
            pl.BlockSpec((1, d), const2),
            pl.BlockSpec(w_pool.shape, lambda i: (0, 0, 0)),
            pl.BlockSpec((1, d), const2),
            pl.BlockSpec((1, d), const2),
        ],
        out_specs=[pl.BlockSpec((tm, d), row), pl.BlockSpec((tm, d), row)],
        out_shape=[jax.ShapeDtypeStruct((s, d), F32), jax.ShapeDtypeStruct((s, d), BF16)],
        compiler_params=_params("arbitrary"),
        name="pool_mixer",
    )(x, x, g, w_pool, scale, g_next)


def _norm_kernel(x_ref, *refs):
    n = len(refs) // 2
    x = x_ref[...]
    xn = x * lax.rsqrt(jnp.mean(x * x, axis=-1, keepdims=True) + EPS)
    for g_ref, o_ref in zip(refs[:n], refs[n:]):
        o_ref[...] = (xn * g_ref[...]).astype(BF16)


def _norm_cast(x, gains):
    s, d = x.shape
    tm = _tile(s, 512)
    row = lambda i: (i, 0)
    outs = pl.pallas_call(
        _norm_kernel,
        grid=(s // tm,),
        in_specs=[pl.BlockSpec((tm, d), row)] + [pl.BlockSpec((1, d), lambda i: (0, 0))] * len(gains),
        out_specs=[pl.BlockSpec((tm, d), row)] * len(gains),
        out_shape=[jax.ShapeDtypeStruct((s, d), BF16)] * len(gains),
        compiler_params=_params("arbitrary"),
        name="rmsnorm_cast",
    )(x, *gains)
    return outs


def _causal_conv(u, prev, cw, b):
    tm = u.shape[0]
    ext = jnp.concatenate([prev, u], axis=0)
    u1 = pltpu.roll(ext, 1, axis=0)[CARRY_ROWS:]
    u2 = pltpu.roll(ext, 2, axis=0)[CARRY_ROWS:]
    return cw[2:3] * u + cw[1:2] * u1 + cw[0:1] * u2 + b, ext[tm:]


def _ffn_in_kernel(a_ref, wg_ref, wv_ref, cwg_ref, cwv_ref, bg_ref, bv_ref, o_ref, carry_ref):
    i = pl.program_id(0)
    j = pl.program_id(1)

    @pl.when(i == 0)
    def _():
        carry_ref[j] = jnp.zeros(carry_ref.shape[1:], F32)

    a = a_ref[...]
    ug = jnp.dot(a, wg_ref[...], preferred_element_type=F32)
    uv = jnp.dot(a, wv_ref[...], preferred_element_type=F32)
    yg, tail_g = _causal_conv(ug, carry_ref[j, 0], cwg_ref[...], bg_ref[...])
    yv, tail_v = _causal_conv(uv, carry_ref[j, 1], cwv_ref[...], bv_ref[...])
    carry_ref[j, 0] = tail_g
    carry_ref[j, 1] = tail_v
    o_ref[...] = (yg * jax.nn.sigmoid(yg) * yv).astype(BF16)


def _ffn_in(h, w_in, conv_w, conv_b):
    s, d = h.shape
    fp = w_in.shape[1] // 2
    tm = _tile(s, 1024)
    tn = _tile(fp, 512)
    nj = fp // tn
    gate = lambda i, j: (0, j)
    val = lambda i, j: (0, j + nj)
    return pl.pallas_call(
        _ffn_in_kernel,
        grid=(s // tm, nj),
        in_specs=[
            pl.BlockSpec((tm, d), lambda i, j: (i, 0)),
            pl.BlockSpec((d, tn), gate),
            pl.BlockSpec((d, tn), val),
            pl.BlockSpec((CONV_WIDTH, tn), gate),
            pl.BlockSpec((CONV_WIDTH, tn), val),
            pl.BlockSpec((1, tn), gate),
            pl.BlockSpec((1, tn), val),
        ],
        out_specs=pl.BlockSpec((tm, tn), lambda i, j: (i, j)),
        out_shape=jax.ShapeDtypeStruct((s, fp), BF16),
        scratch_shapes=[pltpu.VMEM((nj, 2, CARRY_ROWS, tn), F32)],
        compiler_params=_params("arbitrary", "arbitrary"),
        name="ffn_in_conv_act",
    )(h, w_in, w_in, conv_w, conv_w, conv_b, conv_b)


def _mm_res_kernel(a_ref, w_ref, r_ref, o_ref, acc_ref, *, nk):
    k = pl.program_id(2)
    prod = jnp.dot(a_ref[...], w_ref[...], preferred_element_type=F32)
    if nk == 1:
        o_ref[...] = r_ref[...] + prod
        return

    @pl.when(k == 0)
    def _():
        acc_ref[...] = prod

    @pl.when(jnp.logical_and(k > 0, k < nk - 1))
    def _():
        acc_ref[...] += prod

    @pl.when(k == nk - 1)
    def _():
        o_ref[...] = r_ref[...] + (acc_ref[...] + prod)


def _mm_res(a, w, res, tk_want):
    s, kdim = a.shape
    n = w.shape[1]
    tm = _tile(s, 1024)
    tn = _tile(n, 1024)
    tk = _tile(kdim, tk_want)
    nk = kdim // tk
    return pl.pallas_call(
        functools.partial(_mm_res_kernel, nk=nk),
        grid=(s // tm, n // tn, nk),
        in_specs=[
            pl.BlockSpec((tm, tk), lambda i, j, k: (i, k)),
            pl.BlockSpec((tk, tn), lambda i, j, k: (k, j)),
            pl.BlockSpec((tm, tn), lambda i, j, k: (i, j)),
        ],
        out_specs=pl.BlockSpec((tm, tn), lambda i, j, k: (i, j)),
        out_shape=jax.ShapeDtypeStruct((s, n), F32),
        scratch_shapes=[pltpu.VMEM((tm, tn), F32)],
        compiler_params=_params("arbitrary", "arbitrary", "arbitrary"),
        name="matmul_residual",
    )(a, w, res)


def _ple_kernel(h_ref, wg_ref, p_ref, wp_ref, x_ref, o_ref):
    z = jnp.dot(h_ref[...], wg_ref[...], preferred_element_type=F32)
    e = jnp.dot(p_ref[...].astype(BF16), wp_ref[...], preferred_element_type=F32)
    o_ref[...] = x_ref[...] + jax.nn.sigmoid(z) * e


def _ple(hn, w_gate, p, w_ple, x):
    s, d = x.shape
    pd = p.shape[1]
    tm = _tile(s, 1024)
    tn = _tile(d, 512)
    return pl.pallas_call(
        _ple_kernel,
        grid=(s // tm, d // tn),
        in_specs=[
            pl.BlockSpec((tm, d), lambda i, j: (i, 0)),
            pl.BlockSpec((d, tn), lambda i, j: (0, j)),
            pl.BlockSpec((tm, pd), lambda i, j: (i, 0)),
            pl.BlockSpec((pd, tn), lambda i, j: (0, j)),
            pl.BlockSpec((tm, tn), lambda i, j: (i, j)),
        ],
        out_specs=pl.BlockSpec((tm, tn), lambda i, j: (i, j)),
        out_shape=jax.ShapeDtypeStruct((s, d), F32),
        compiler_params=_params("arbitrary", "arbitrary"),
        name="per_layer_embed",
    )(hn, w_gate, p, w_ple, x)


def _proj_kernel(a_ref, w_ref, g_ref, o_ref, *, n_norm_tiles, out_scale):
    j = pl.program_id(1)
    z = jnp.dot(a_ref[...], w_ref[...], preferred_element_type=F32)
    tn = z.shape[1]

    @pl.when(j < n_norm_tiles)
    def _():
        g = g_ref[...] * out_scale
        for hh in range(tn // HEAD_DIM):
            sl = slice(hh * HEAD_DIM, (hh + 1) * HEAD_DIM)
            zh = z[:, sl]
            ms = jnp.mean(zh * zh, axis=-1, keepdims=True)
            o_ref[:, sl] = (zh * lax.rsqrt(ms + EPS) * g).astype(BF16)

    @pl.when(j >= n_norm_tiles)
    def _():
        o_ref[...] = z.astype(BF16)


def _proj_heads(a, w, head_gain, n_norm_cols, out_scale):
    s, d = a.shape
    n = w.shape[1]
    tm = _tile(s, 1024)
    tn = _tile(n, 512)
    return pl.pallas_call(
        functools.partial(_proj_kernel, n_norm_tiles=n_norm_cols // tn, out_scale=out_scale),
        grid=(s // tm, n // tn),
        in_specs=[
            pl.BlockSpec((tm, d), lambda i, j: (i, 0)),
            pl.BlockSpec((d, tn), lambda i, j: (0, j)),
            pl.BlockSpec((1, HEAD_DIM), lambda i, j: (0, 0)),
        ],
        out_specs=pl.BlockSpec((tm, tn), lambda i, j: (i, j)),
        out_shape=jax.ShapeDtypeStruct((s, n), BF16),
        compiler_params=_params("arbitrary", "arbitrary"),
        name="head_projection",
    )(a, w, head_gain)


def _gate_kernel(a_ref, w_ref, b_ref, o_ref, carry_ref, *, tm):
    i = pl.program_id(0)

    @pl.when(i == 0)
    def _():
        carry_ref[...] = jnp.zeros(carry_ref.shape, F32)

    z = jnp.dot(a_ref[...], w_ref[...], preferred_element_type=F32) + b_ref[...]
    run = jnp.minimum(z, 0.0) - jnp.log1p(jnp.exp(-jnp.abs(z)))
    row = lax.broadcasted_iota(jnp.int32, run.shape, 0)
    shift = 1
    while shift < tm:
        run = run + jnp.where(row >= shift, pltpu.roll(run, shift, axis=0), 0.0)
        shift *= 2
    run = run + carry_ref[...]
    o_ref[...] = run
    carry_ref[...] = run[tm - 1:tm, :]


def _forget_cumsum(a, w_f, b_f):
    s, d = a.shape
    n = w_f.shape[1]
    tm = _tile(s, 512)
    return pl.pallas_call(
        functools.partial(_gate_kernel, tm=tm),
        grid=(s // tm,),
        in_specs=[
            pl.BlockSpec((tm, d), lambda i: (i, 0)),
            pl.BlockSpec((d, n), lambda i: (0, 0)),
            pl.BlockSpec((1, n), lambda i: (0, 0)),
        ],
        out_specs=pl.BlockSpec((tm, n), lambda i: (i, 0)),
        out_shape=jax.ShapeDtypeStruct((s, n), F32),
        scratch_shapes=[pltpu.VMEM((1, n), F32)],
        compiler_params=_params("arbitrary"),
        name="forget_gate_cumsum",
    )(a, w_f, b_f)


GATE_PIECES = 3
V_ROWS = HEAD_DIM + 16
QUERY_TILE = 256


def _attn_kernel(qt_ref, k_ref, vt_ref, o_ref, m_ref, acc_ref, *, tq, tk):
    i = pl.program_id(1)
    m_ref[...] = jnp.full(m_ref.shape, -jnp.inf, F32)
    acc_ref[...] = jnp.zeros(acc_ref.shape, F32)

    def scores(n, k_rows):
        cols = slice(n * QUERY_TILE, (n + 1) * QUERY_TILE)
        return jnp.dot(k_rows, qt_ref[0, :, cols], preferred_element_type=F32)

    def attend(n, st, vt_cols, diag_off):
        cols = slice(n * QUERY_TILE, (n + 1) * QUERY_TILE)
        if diag_off is not None:
            key = lax.broadcasted_iota(jnp.int32, st.shape, 0)
            qry = lax.broadcasted_iota(jnp.int32, st.shape, 1) + diag_off
            st = jnp.where(key <= qry, st, -jnp.inf)
        m_old = m_ref[:, cols]
        m_new = jnp.maximum(m_old, jnp.max(st, axis=0, keepdims=True))
        pt = jnp.exp2(st - m_new).astype(BF16)
        alpha = jnp.exp2(m_old - m_new)
        acc_ref[:, cols] = alpha * acc_ref[:, cols] + jnp.dot(vt_cols, pt, preferred_element_type=F32)
        m_ref[:, cols] = m_new

    def body(j, carry):
        start = pl.multiple_of(j * tk, tk)
        k_rows = k_ref[0, pl.ds(start, tk), :]
        vt_cols = vt_ref[0, :, pl.ds(start, tk)]
        st = scores(0, k_rows)
        for n in range(nt):
            st_next = scores(n + 1, k_rows) if n + 1 < nt else None
            attend(n, st, vt_cols, None)
            st = st_next
        return carry

    nt = tq // QUERY_TILE
    lax.fori_loop(0, (i * tq) // tk, body, 0)
    base = pl.multiple_of(i * tq, tq)
    diag_keys = lambda n: k_ref[0, pl.ds(base, (n + 1) * QUERY_TILE), :]
    st = scores(0, diag_keys(0))
    for n in range(nt):
        st_next = scores(n + 1, diag_keys(n + 1)) if n + 1 < nt else None
        attend(n, st, vt_ref[0, :, pl.ds(base, (n + 1) * QUERY_TILE)], n * QUERY_TILE)
        st = st_next
    inv_l = 1.0 / acc_ref[HEAD_DIM:HEAD_DIM + 1, :]
    o_ref[0] = (acc_ref[:HEAD_DIM, :] * inv_l).astype(BF16)


def _split_bf16(t):
    pieces = []
    for _ in range(GATE_PIECES):
        top = lax.bitcast_convert_type(
            lax.bitcast_convert_type(t, jnp.uint32) & jnp.uint32(0xFFFF0000), F32)
        pieces.append(top.astype(BF16))
        t = t - top
    return jnp.stack(pieces, axis=1)


def _attention(q, k, v, c):
    s, d = q.shape
    nh = d // HEAD_DIM
    heads_t = lambda t: t.reshape(s, nh, HEAD_DIM).transpose(1, 2, 0)
    cp = _split_bf16(c * LOG2E)
    ones = jnp.ones_like(cp)
    fill = jnp.zeros((nh, HEAD_DIM - 2 * GATE_PIECES, s), BF16)
    qt = jnp.concatenate([heads_t(q), -ones, cp, fill], axis=1)
    ka = jnp.concatenate([heads_t(k), cp, ones, fill], axis=1).transpose(0, 2, 1)
    vt = jnp.concatenate([heads_t(v), jnp.ones((nh, 1, s), BF16),
                          jnp.zeros((nh, V_ROWS - HEAD_DIM - 1, s), BF16)], axis=1)
    tq = _tile(s, 1024)
    tk = _tile(tq, 1024)
    assert tq % QUERY_TILE == 0
    out_t = pl.pallas_call(
        functools.partial(_attn_kernel, tq=tq, tk=tk),
        grid=(nh, s // tq),
        in_specs=[
            pl.BlockSpec((1, 2 * HEAD_DIM, tq), lambda h, i: (h, 0, i)),
            pl.BlockSpec((1, s, 2 * HEAD_DIM), lambda h, i: (h, 0, 0)),
            pl.BlockSpec((1, V_ROWS, s), lambda h, i: (h, 0, 0)),
        ],
        out_specs=pl.BlockSpec((1, HEAD_DIM, tq), lambda h, i: (h, 0, i)),
        out_shape=jax.ShapeDtypeStruct((nh, HEAD_DIM, s), BF16),
        scratch_shapes=[
            pltpu.VMEM((1, tq), F32),
            pltpu.VMEM((V_ROWS, tq), F32),
        ],
        compiler_params=_params("arbitrary", "arbitrary"),
        name="forgetting_attention",
    )(qt, ka, vt)
    return out_t.transpose(2, 0, 1).reshape(s, d)


def _pad_ff(t, f, fp, axis):
    pad = [(0, 0)] * t.ndim
    pad[axis] = (0, fp - f)
    return jnp.pad(t, pad)


def _split_pad(t, f, fp):
    return jnp.concatenate([_pad_ff(t[..., :f], f, fp, -1), _pad_ff(t[..., f:], f, fp, -1)], axis=-1)


def _conv_ffn(x, hn, w_in, conv_w, conv_b, w_out):
    f = w_out.shape[0]
    fp = -(-f // FF_ALIGN) * FF_ALIGN
    act = _ffn_in(hn, _split_pad(w_in, f, fp).astype(BF16), _split_pad(conv_w, f, fp),
                  _split_pad(conv_b[None, :], f, fp))
    return _mm_res(act, _pad_ff(w_out, f, fp, 0).astype(BF16), x, fp // 4)


def kernel(x, p, pool_norm, w_pool, pool_scale, attn_norm, w_q, q_norm, w_o, kv_norm, w_kvf, b_f,
           k_norm, ffn_norm, w_in, conv_w, conv_b, w_out, w_ple, gate_norm, w_gate):
    b, s, d = x.shape
    assert b == 1 and d % (HEAD_DIM * len(POOL_WINDOWS)) == 0
    depth = p.shape[0]
    n_a = pool_norm.shape[0]
    nh = d // HEAD_DIM
    x = x[0]
    row = lambda t: t[None, :]
    kv = c_heads = None
    for i in range(depth):
        if i < n_a:
            x, hn = _pool_layer(x, row(pool_norm[i]), w_pool[i].astype(BF16), row(pool_scale[i]),
                                row(ffn_norm[i]))
        else:
            j = i - n_a
            (hq,) = _norm_cast(x, [row(attn_norm[j])])
            q = _proj_heads(hq, w_q[j].astype(BF16), row(q_norm[j]), d, HEAD_DIM ** -0.5 * LOG2E)
            att = _attention(q, kv[:, :d], kv[:, d:], c_heads)
            x = _mm_res(att, w_o[j].astype(BF16), x, d)
            (hn,) = _norm_cast(x, [row(ffn_norm[i])])
        x = _conv_ffn(x, hn, w_in[i], conv_w[i], conv_b[i], w_out[i])
        (hg,) = _norm_cast(x, [row(gate_norm[i])])
        x = _ple(hg, w_gate[i].astype(BF16), p[i, 0], w_ple[i].astype(BF16), x)
        if i == n_a - 1:
            (hk,) = _norm_cast(x, [row(kv_norm)])
            kv = _proj_heads(hk, w_kvf[:, :2 * d].astype(BF16), row(k_norm), d, 1.0)
            w_f = jnp.pad(w_kvf[:, 2 * d:], ((0, 0), (0, LANE - nh))).astype(BF16)
            c = _forget_cumsum(hk, w_f, jnp.pad(b_f, (0, LANE - nh))[None, :])
            c_heads = c[:, :nh].T
    return x[None]
```

```python
import functools
import math

import jax
import jax.numpy as jnp
from jax import lax
from jax.experimental import pallas as pl
from jax.experimental.pallas import tpu as pltpu

HEAD_DIM = 128
POOL_WINDOWS = (2, 4, 8, 16)
POOL_HALO = 16
CONV_WIDTH = 3
CARRY_ROWS = 8
EPS = 1e-6
LOG2E = math.log2(math.e)
LANE = 128
FF_ALIGN = 1024
VMEM_LIMIT = 56 * 1024 * 1024

F32 = jnp.float32
BF16 = jnp.bfloat16


def _tile(n, want):
    t = min(n, want)
    while n % t:
        t //= 2
    return t


def _params(*sem):
    return pltpu.CompilerParams(dimension_semantics=sem, vmem_limit_bytes=VMEM_LIMIT)


def _rms(x, g):
    ms = jnp.mean(x * x, axis=-1, keepdims=True)
    return x * lax.rsqrt(ms + EPS) * g


def _pool_kernel(x_ref, halo_ref, g_ref, w_ref, sc_ref, g2_ref, xo_ref, ho_ref, *, tm):
    i = pl.program_id(0)
    x = x_ref[...]
    g = g_ref[...]
    h = _rms(x, g)
    hh = jnp.where(i > 0, _rms(halo_ref[...], g), 0.0)
    cur = jnp.concatenate([hh, h], axis=0)
    d = x.shape[1]
    c = d // len(POOL_WINDOWS)
    pos = i * tm + lax.broadcasted_iota(jnp.int32, (tm, 1), 0)
    outs = []
    for gi, w in enumerate(POOL_WINDOWS):
        cur = cur + pltpu.roll(cur, w // 2, axis=0)
        win = cur[POOL_HALO:, :c]
        if gi + 1 < len(POOL_WINDOWS):
            cur = cur[:, c:]
        inv_cnt = 1.0 / jnp.minimum(pos + 1, w).astype(F32)
        pooled = win * inv_cnt - h[:, gi * c:(gi + 1) * c]
        outs.append(jnp.dot(pooled.astype(BF16), w_ref[gi], preferred_element_type=F32))
    x1 = x + jnp.concatenate(outs, axis=1) * sc_ref[...]
    xo_ref[...] = x1
    ho_ref[...] = _rms(x1, g2_ref[...]).astype(BF16)


def _pool_layer(x, g, w_pool, scale, g_next):
    s, d = x.shape
    tm = _tile(s, 256)
    hb = tm // POOL_HALO
    row = lambda i: (i, 0)
    const2 = lambda i: (0, 0)
    return pl.pallas_call(
        functools.partial(_pool_kernel, tm=tm),
        grid=(s // tm,),
        in_specs=[
            pl.BlockSpec((tm, d), row),
            pl.BlockSpec((POOL_HALO, d), lambda i: (jnp.maximum(i * hb - 1, 0), 0)),
            pl.BlockSpec((1, d), const2),
            pl.BlockSpec(w_pool.shape, lambda i: (0, 0, 0)),
            pl.BlockSpec((1, d), const2),
            pl.BlockSpec((1, d), const2),
        ],
        out_specs=[pl.BlockSpec((tm, d), row), pl.BlockSpec((tm, d), row)],
        out_shape=[jax.ShapeDtypeStruct((s, d), F32), jax.ShapeDtypeStruct((s, d), BF16)],
        compiler_params=_params("arbitrary"),
        name="pool_mixer",
    )(x, x, g, w_pool, scale, g_next)


def _norm_kernel(x_ref, *refs):
    n = len(refs) // 2
    x = x_ref[...]
    xn = x * lax.rsqrt(jnp.mean(x * x, axis=-1, keepdims=True) + EPS)
    for g_ref, o_ref in zip(refs[:n], refs[n:]):
        o_ref[...] = (xn * g_ref[...]).astype(BF16)


def _norm_cast(x, gains):
    s, d = x.shape
    tm = _tile(s, 512)
    row = lambda i: (i, 0)
    outs = pl.pallas_call(
        _norm_kernel,
        grid=(s // tm,),
        in_specs=[pl.BlockSpec((tm, d), row)] + [pl.BlockSpec((1, d), lambda i: (0, 0))] * len(gains),
        out_specs=[pl.BlockSpec((tm, d), row)] * len(gains),
        out_shape=[jax.ShapeDtypeStruct((s, d), BF16)] * len(gains),
        compiler_params=_params("arbitrary"),
        name="rmsnorm_cast",
    )(x, *gains)
    return outs


def _causal_conv(u, prev, cw, b):
    tm = u.shape[0]
    ext = jnp.concatenate([prev, u], axis=0)
    u1 = pltpu.roll(ext, 1, axis=0)[CARRY_ROWS:]
    u2 = pltpu.roll(ext, 2, axis=0)[CARRY_ROWS:]
    return cw[2:3] * u + cw[1:2] * u1 + cw[0:1] * u2 + b, ext[tm:]


def _ffn_in_kernel(a_ref, wg_ref, wv_ref, cwg_ref, cwv_ref, bg_ref, bv_ref, o_ref, carry_ref):
    i = pl.program_id(0)
    j = pl.program_id(1)

    @pl.when(i == 0)
    def _():
        carry_ref[j] = jnp.zeros(carry_ref.shape[1:], F32)

    a = a_ref[...]
    ug = jnp.dot(a, wg_ref[...], preferred_element_type=F32)
    uv = jnp.dot(a, wv_ref[...], preferred_element_type=F32)
    yg, tail_g = _causal_conv(ug, carry_ref[j, 0], cwg_ref[...], bg_ref[...])
    yv, tail_v = _causal_conv(uv, carry_ref[j, 1], cwv_ref[...], bv_ref[...])
    carry_ref[j, 0] = tail_g
    carry_ref[j, 1] = tail_v
    o_ref[...] = (yg * jax.nn.sigmoid(yg) * yv).astype(BF16)


def _ffn_in(h, w_in, conv_w, conv_b):
    s, d = h.shape
    fp = w_in.shape[1] // 2
    tm = _tile(s, 1024)
    tn = _tile(fp, 512)
    nj = fp // tn
    gate = lambda i, j: (0, j)
    val = lambda i, j: (0, j + nj)
    return pl.pallas_call(
        _ffn_in_kernel,
        grid=(s // tm, nj),
        in_specs=[
            pl.BlockSpec((tm, d), lambda i, j: (i, 0)),
            pl.BlockSpec((d, tn), gate),
            pl.BlockSpec((d, tn), val),
            pl.BlockSpec((CONV_WIDTH, tn), gate),
            pl.BlockSpec((CONV_WIDTH, tn), val),
            pl.BlockSpec((1, tn), gate),
            pl.BlockSpec((1, tn), val),
        ],
        out_specs=pl.BlockSpec((tm, tn), lambda i, j: (i, j)),
        out_shape=jax.ShapeDtypeStruct((s, fp), BF16),
        scratch_shapes=[pltpu.VMEM((nj, 2, CARRY_ROWS, tn), F32)],
        compiler_params=_params("arbitrary", "arbitrary"),
        name="ffn_in_conv_act",
    )(h, w_in, w_in, conv_w, conv_w, conv_b, conv_b)


def _mm_res_kernel(a_ref, w_ref, r_ref, o_ref, acc_ref, *, nk):
    k = pl.program_id(2)
    prod = jnp.dot(a_ref[...], w_ref[...], preferred_element_type=F32)
    if nk == 1:
        o_ref[...] = r_ref[...] + prod
        return

    @pl.when(k == 0)
    def _():
        acc_ref[...] = prod

    @pl.when(jnp.logical_and(k > 0, k < nk - 1))
    def _():
        acc_ref[...] += prod

    @pl.when(k == nk - 1)
    def _():
        o_ref[...] = r_ref[...] + (acc_ref[...] + prod)


def _mm_res(a, w, res, tk_want):
    s, kdim = a.shape
    n = w.shape[1]
    tm = _tile(s, 1024)
    tn = _tile(n, 1024)
    tk = _tile(kdim, tk_want)
    nk = kdim // tk
    return pl.pallas_call(
        functools.partial(_mm_res_kernel, nk=nk),
        grid=(s // tm, n // tn, nk),
        in_specs=[
            pl.BlockSpec((tm, tk), lambda i, j, k: (i, k)),
            pl.BlockSpec((tk, tn), lambda i, j, k: (k, j)),
            pl.BlockSpec((tm, tn), lambda i, j, k: (i, j)),
        ],
        out_specs=pl.BlockSpec((tm, tn), lambda i, j, k: (i, j)),
        out_shape=jax.ShapeDtypeStruct((s, n), F32),
        scratch_shapes=[pltpu.VMEM((tm, tn), F32)],
        compiler_params=_params("arbitrary", "arbitrary", "arbitrary"),
        name="matmul_residual",
    )(a, w, res)


def _ple_kernel(h_ref, wg_ref, p_ref, wp_ref, x_ref, o_ref):
    z = jnp.dot(h_ref[...], wg_ref[...], preferred_element_type=F32)
    e = jnp.dot(p_ref[...].astype(BF16), wp_ref[...], preferred_element_type=F32)
    o_ref[...] = x_ref[...] + jax.nn.sigmoid(z) * e


def _ple(hn, w_gate, p, w_ple, x):
    s, d = x.shape
    pd = p.shape[1]
    tm = _tile(s, 1024)
    tn = _tile(d, 512)
    return pl.pallas_call(
        _ple_kernel,
        grid=(s // tm, d // tn),
        in_specs=[
            pl.BlockSpec((tm, d), lambda i, j: (i, 0)),
            pl.BlockSpec((d, tn), lambda i, j: (0, j)),
            pl.BlockSpec((tm, pd), lambda i, j: (i, 0)),
            pl.BlockSpec((pd, tn), lambda i, j: (0, j)),
            pl.BlockSpec((tm, tn), lambda i, j: (i, j)),
        ],
        out_specs=pl.BlockSpec((tm, tn), lambda i, j: (i, j)),
        out_shape=jax.ShapeDtypeStruct((s, d), F32),
        compiler_params=_params("arbitrary", "arbitrary"),
        name="per_layer_embed",
    )(hn, w_gate, p, w_ple, x)


def _proj_kernel(a_ref, w_ref, g_ref, o_ref, *, n_norm_tiles, out_scale):
    j = pl.program_id(1)
    z = jnp.dot(a_ref[...], w_ref[...], preferred_element_type=F32)
    tn = z.shape[1]

    @pl.when(j < n_norm_tiles)
    def _():
        g = g_ref[...] * out_scale
        for hh in range(tn // HEAD_DIM):
            sl = slice(hh * HEAD_DIM, (hh + 1) * HEAD_DIM)
            zh = z[:, sl]
            ms = jnp.mean(zh * zh, axis=-1, keepdims=True)
            o_ref[:, sl] = (zh * lax.rsqrt(ms + EPS) * g).astype(BF16)

    @pl.when(j >= n_norm_tiles)
    def _():
        o_ref[...] = z.astype(BF16)


def _proj_heads(a, w, head_gain, n_norm_cols, out_scale):
    s, d = a.shape
    n = w.shape[1]
    tm = _tile(s, 1024)
    tn = _tile(n, 512)
    return pl.pallas_call(
        functools.partial(_proj_kernel, n_norm_tiles=n_norm_cols // tn, out_scale=out_scale),
        grid=(s // tm, n // tn),
        in_specs=[
            pl.BlockSpec((tm, d), lambda i, j: (i, 0)),
            pl.BlockSpec((d, tn), lambda i, j: (0, j)),
            pl.BlockSpec((1, HEAD_DIM), lambda i, j: (0, 0)),
        ],
        out_specs=pl.BlockSpec((tm, tn), lambda i, j: (i, j)),
        out_shape=jax.ShapeDtypeStruct((s, n), BF16),
        compiler_params=_params("arbitrary", "arbitrary"),
        name="head_projection",
    )(a, w, head_gain)


def _gate_kernel(a_ref, w_ref, b_ref, o_ref, carry_ref, *, tm):
    i = pl.program_id(0)

    @pl.when(i == 0)
    def _():
        carry_ref[...] = jnp.zeros(carry_ref.shape, F32)

    z = jnp.dot(a_ref[...], w_ref[...], preferred_element_type=F32) + b_ref[...]
    run = jnp.minimum(z, 0.0) - jnp.log1p(jnp.exp(-jnp.abs(z)))
    row = lax.broadcasted_iota(jnp.int32, run.shape, 0)
    shift = 1
    while shift < tm:
        run = run + jnp.where(row >= shift, pltpu.roll(run, shift, axis=0), 0.0)
        shift *= 2
    run = run + carry_ref[...]
    o_ref[...] = run
    carry_ref[...] = run[tm - 1:tm, :]


def _forget_cumsum(a, w_f, b_f):
    s, d = a.shape
    n = w_f.shape[1]
    tm = _tile(s, 512)
    return pl.pallas_call(
        functools.partial(_gate_kernel, tm=tm),
        grid=(s // tm,),
        in_specs=[
            pl.BlockSpec((tm, d), lambda i: (i, 0)),
            pl.BlockSpec((d, n), lambda i: (0, 0)),
            pl.BlockSpec((1, n), lambda i: (0, 0)),
        ],
        out_specs=pl.BlockSpec((tm, n), lambda i: (i, 0)),
        out_shape=jax.ShapeDtypeStruct((s, n), F32),
        scratch_shapes=[pltpu.VMEM((1, n), F32)],
        compiler_params=_params("arbitrary"),
        name="forget_gate_cumsum",
    )(a, w_f, b_f)


GATE_PIECES = 3
V_ROWS = HEAD_DIM + 16
QUERY_TILE = 512


def _transposed_bf16(t):
    return t.astype(F32).T.astype(BF16)


def _attn_kernel(q_ref, qc_ref, k_ref, kc_ref, v_ref, o_ref, m_ref, acc_ref, qt_ref, vt_ref, *, tq, tk):
    i = pl.program_id(1)
    seq = v_ref.shape[0]

    @pl.when(i == 0)
    def _():
        def put(cidx, carry):
            at = pl.multiple_of(cidx * tk, tk)
            vt_ref[:HEAD_DIM, pl.ds(at, tk)] = _transposed_bf16(v_ref[pl.ds(at, tk), :])
            return carry

        lax.fori_loop(0, seq // tk, put, 0)
        extra = lax.broadcasted_iota(jnp.int32, (V_ROWS - HEAD_DIM, seq), 0)
        vt_ref[HEAD_DIM:, :] = jnp.where(extra == 0, 1.0, 0.0).astype(BF16)

    qt_ref[:HEAD_DIM, :] = _transposed_bf16(q_ref[...])
    qt_ref[HEAD_DIM:, :] = _transposed_bf16(qc_ref[0])
    m_ref[...] = jnp.full(m_ref.shape, -jnp.inf, F32)
    acc_ref[...] = jnp.zeros(acc_ref.shape, F32)

    def keys(start, rows):
        return jnp.concatenate([k_ref[pl.ds(start, rows), :], kc_ref[0, pl.ds(start, rows), :]], axis=1)

    def scores(n, k_rows):
        cols = slice(n * QUERY_TILE, (n + 1) * QUERY_TILE)
        return jnp.dot(k_rows, qt_ref[:, cols], preferred_element_type=F32)

    def attend(n, st, vt_cols, diag_off):
        cols = slice(n * QUERY_TILE, (n + 1) * QUERY_TILE)
        if diag_off is not None:
            key = lax.broadcasted_iota(jnp.int32, st.shape, 0)
            qry = lax.broadcasted_iota(jnp.int32, st.shape, 1) + diag_off
            st = jnp.where(key <= qry, st, -jnp.inf)
        m_old = m_ref[:, cols]
        m_new = jnp.maximum(m_old, jnp.max(st, axis=0, keepdims=True))
        pt = jnp.exp2(st - m_new).astype(BF16)
        alpha = jnp.exp2(m_old - m_new)
        acc_ref[:, cols] = alpha * acc_ref[:, cols] + jnp.dot(vt_cols, pt, preferred_element_type=F32)
        m_ref[:, cols] = m_new

    def body(j, carry):
        start = pl.multiple_of(j * tk, tk)
        k_rows = keys(start, tk)
        vt_cols = vt_ref[:, pl.ds(start, tk)]
        st = scores(0, k_rows)
        for n in range(nt):
            st_next = scores(n + 1, k_rows) if n + 1 < nt else None
            attend(n, st, vt_cols, None)
            st = st_next
        return carry

    nt = tq // QUERY_TILE
    lax.fori_loop(0, (i * tq) // tk, body, 0)
    base = pl.multiple_of(i * tq, tq)
    diag_keys = lambda n: keys(base, (n + 1) * QUERY_TILE)
    st = scores(0, diag_keys(0))
    for n in range(nt):
        st_next = scores(n + 1, diag_keys(n + 1)) if n + 1 < nt else None
        attend(n, st, vt_ref[:, pl.ds(base, (n + 1) * QUERY_TILE)], n * QUERY_TILE)
        st = st_next
    inv_l = 1.0 / acc_ref[HEAD_DIM:HEAD_DIM + 1, :]
    o_ref[...] = (acc_ref[:HEAD_DIM, :] * inv_l).T.astype(BF16)


def _split_bf16(t):
    pieces = []
    for _ in range(GATE_PIECES):
        top = lax.bitcast_convert_type(
            lax.bitcast_convert_type(t, jnp.uint32) & jnp.uint32(0xFFFF0000), F32)
        pieces.append(top.astype(BF16))
        t = t - top
    return pieces


def _gate_columns(c, gate_first):
    cp = jnp.stack(_split_bf16(c * LOG2E), axis=-1)
    const = jnp.full(cp.shape, 1.0 if gate_first else -1.0, BF16)
    cols = jnp.concatenate([cp, const] if gate_first else [const, cp], axis=-1)
    cols = jnp.pad(cols, ((0, 0), (0, 0), (0, HEAD_DIM - 2 * GATE_PIECES)))
    return cols.transpose(1, 0, 2)


def _attention(q, kv, c):
    s, d = q.shape
    nh = d // HEAD_DIM
    tq = _tile(s, 2048)
    tk = _tile(tq, 1024)
    assert tq % QUERY_TILE == 0
    return pl.pallas_call(
        functools.partial(_attn_kernel, tq=tq, tk=tk),
        grid=(nh, s // tq),
        in_specs=[
            pl.BlockSpec((tq, HEAD_DIM), lambda h, i: (i, h)),
            pl.BlockSpec((1, tq, HEAD_DIM), lambda h, i: (h, i, 0)),
            pl.BlockSpec((s, HEAD_DIM), lambda h, i: (0, h)),
            pl.BlockSpec((1, s, HEAD_DIM), lambda h, i: (h, 0, 0)),
            pl.BlockSpec((s, HEAD_DIM), lambda h, i: (0, nh + h)),
        ],
        out_specs=pl.BlockSpec((tq, HEAD_DIM), lambda h, i: (i, h)),
        out_shape=jax.ShapeDtypeStruct((s, d), BF16),
        scratch_shapes=[
            pltpu.VMEM((1, tq), F32),
            pltpu.VMEM((V_ROWS, tq), F32),
            pltpu.VMEM((2 * HEAD_DIM, tq), BF16),
            pltpu.VMEM((V_ROWS, s), BF16),
        ],
        compiler_params=_params("arbitrary", "arbitrary"),
        name="forgetting_attention",
    )(q, _gate_columns(c, False), kv, _gate_columns(c, True), kv)


def _pad_ff(t, f, fp, axis):
    pad = [(0, 0)] * t.ndim
    pad[axis] = (0, fp - f)
    return jnp.pad(t, pad)


def _split_pad(t, f, fp):
    return jnp.concatenate([_pad_ff(t[..., :f], f, fp, -1), _pad_ff(t[..., f:], f, fp, -1)], axis=-1)


def _conv_ffn(x, hn, w_in, conv_w, conv_b, w_out):
    f = w_out.shape[0]
    fp = -(-f // FF_ALIGN) * FF_ALIGN
    act = _ffn_in(hn, _split_pad(w_in, f, fp).astype(BF16), _split_pad(conv_w, f, fp),
                  _split_pad(conv_b[None, :], f, fp))
    return _mm_res(act, _pad_ff(w_out, f, fp, 0).astype(BF16), x, fp // 4)


def kernel(x, p, pool_norm, w_pool, pool_scale, attn_norm, w_q, q_norm, w_o, kv_norm, w_kvf, b_f,
           k_norm, ffn_norm, w_in, conv_w, conv_b, w_out, w_ple, gate_norm, w_gate):
    b, s, d = x.shape
    assert b == 1 and d % (HEAD_DIM * len(POOL_WINDOWS)) == 0
    depth = p.shape[0]
    n_a = pool_norm.shape[0]
    nh = d // HEAD_DIM
    x = x[0]
    row = lambda t: t[None, :]
    kv = c_heads = None
    for i in range(depth):
        if i < n_a:
            x, hn = _pool_layer(x, row(pool_norm[i]), w_pool[i].astype(BF16), row(pool_scale[i]),
                                row(ffn_norm[i]))
        else:
            j = i - n_a
            (hq,) = _norm_cast(x, [row(attn_norm[j])])
            q = _proj_heads(hq, w_q[j].astype(BF16), row(q_norm[j]), d, HEAD_DIM ** -0.5 * LOG2E)
            att = _attention(q, kv, c_heads)
            x = _mm_res(att, w_o[j].astype(BF16), x, d)
            (hn,) = _norm_cast(x, [row(ffn_norm[i])])
        x = _conv_ffn(x, hn, w_in[i], conv_w[i], conv_b[i], w_out[i])
        (hg,) = _norm_cast(x, [row(gate_norm[i])])
        x = _ple(hg, w_gate[i].astype(BF16), p[i, 0], w_ple[i].astype(BF16), x)
        if i == n_a - 1:
            (hk,) = _norm_cast(x, [row(kv_norm)])
            kv = _proj_heads(hk, w_kvf[:, :2 * d].astype(BF16), row(k_norm), d, 1.0)
            w_f = jnp.pad(w_kvf[:, 2 * d:], ((0, 0), (0, LANE - nh))).astype(BF16)
            c = _forget_cumsum(hk, w_f, jnp.pad(b_f, (0, LANE - nh))[None, :])
            c_heads = c[:, :nh]
    return x[None]
```

```python
import functools
import math

import jax
import jax.numpy as jnp
from jax import lax
from jax.experimental import pallas as pl
from jax.experimental.pallas import tpu as pltpu

HEAD_DIM = 128
POOL_WINDOWS = (2, 4, 8, 16)
POOL_HALO = 16
CONV_WIDTH = 3
CARRY_ROWS = 8
EPS = 1e-6
LOG2E = math.log2(math.e)
LANE = 128
FF_ALIGN = 1024
VMEM_LIMIT = 56 * 1024 * 1024

F32 = jnp.float32
BF16 = jnp.bfloat16


def _tile(n, want):
    t = min(n, want)
    while n % t:
        t //= 2
    return t


def _params(*sem):
    return pltpu.CompilerParams(dimension_semantics=sem, vmem_limit_bytes=VMEM_LIMIT)


def _rms(x, g):
    ms = jnp.mean(x * x, axis=-1, keepdims=True)
    return x * lax.rsqrt(ms + EPS) * g


def _rstd(ss_ref, d):
    return lax.rsqrt(ss_ref[:, :1] * (1.0 / d) + EPS)


def _emit_stream(out, j, o_ref, ob_ref, ss_ref):
    o_ref[...] = out
    ob_ref[...] = out.astype(BF16)
    part = jnp.broadcast_to(jnp.sum(out * out, axis=1, keepdims=True), ss_ref.shape)

    @pl.when(j == 0)
    def _():
        ss_ref[...] = part

    @pl.when(j > 0)
    def _():
        ss_ref[...] += part


def _stream_shapes(s, d):
    return [jax.ShapeDtypeStruct((s, d), F32), jax.ShapeDtypeStruct((s, d), BF16),
            jax.ShapeDtypeStruct((s, LANE), F32)]


def _pool_kernel(x_ref, halo_ref, g_ref, w_ref, sc_ref, xo_ref, xb_ref, ss_ref, *, tm):
    i = pl.program_id(0)
    x = x_ref[...]
    g = g_ref[...]
    h = _rms(x, g)
    hh = jnp.where(i > 0, _rms(halo_ref[...], g), 0.0)
    cur = jnp.concatenate([hh, h], axis=0)
    d = x.shape[1]
    c = d // len(POOL_WINDOWS)
    pos = i * tm + lax.broadcasted_iota(jnp.int32, (tm, 1), 0)
    outs = []
    for gi, w in enumerate(POOL_WINDOWS):
        cur = cur + pltpu.roll(cur, w // 2, axis=0)
        win = cur[POOL_HALO:, :c]
        if gi + 1 < len(POOL_WINDOWS):
            cur = cur[:, c:]
        inv_cnt = 1.0 / jnp.minimum(pos + 1, w).astype(F32)
        pooled = win * inv_cnt - h[:, gi * c:(gi + 1) * c]
        outs.append(jnp.dot(pooled.astype(BF16), w_ref[gi], preferred_element_type=F32))
    x1 = x + jnp.concatenate(outs, axis=1) * sc_ref[...]
    xo_ref[...] = x1
    xb_ref[...] = x1.astype(BF16)
    ss_ref[...] = jnp.broadcast_to(jnp.sum(x1 * x1, axis=1, keepdims=True), ss_ref.shape)


def _pool_layer(x, g, w_pool, scale):
    s, d = x.shape
    tm = _tile(s, 256)
    hb = tm // POOL_HALO
    row = lambda i: (i, 0)
    const2 = lambda i: (0, 0)
    return pl.pallas_call(
        functools.partial(_pool_kernel, tm=tm),
        grid=(s // tm,),
        in_specs=[
            pl.BlockSpec((tm, d), row),
            pl.BlockSpec((POOL_HALO, d), lambda i: (jnp.maximum(i * hb - 1, 0), 0)),
            pl.BlockSpec((1, d), const2),
            pl.BlockSpec(w_pool.shape, lambda i: (0, 0, 0)),
            pl.BlockSpec((1, d), const2),
        ],
        out_specs=[pl.BlockSpec((tm, d), row), pl.BlockSpec((tm, d), row), pl.BlockSpec((tm, LANE), row)],
        out_shape=_stream_shapes(s, d),
        compiler_params=_params("arbitrary"),
        name="pool_mixer",
    )(x, x, g, w_pool, scale)


def _causal_conv(u, prev, cw, b):
    tm = u.shape[0]
    ext = jnp.concatenate([prev, u], axis=0)
    u1 = pltpu.roll(ext, 1, axis=0)[CARRY_ROWS:]
    u2 = pltpu.roll(ext, 2, axis=0)[CARRY_ROWS:]
    return cw[2:3] * u + cw[1:2] * u1 + cw[0:1] * u2 + b, ext[tm:]


def _ffn_in_kernel(a_ref, ss_ref, wg_ref, wv_ref, cwg_ref, cwv_ref, bg_ref, bv_ref, o_ref, carry_ref):
    i = pl.program_id(0)
    j = pl.program_id(1)

    @pl.when(i == 0)
    def _():
        carry_ref[j] = jnp.zeros(carry_ref.shape[1:], F32)

    a = a_ref[...]
    rstd = _rstd(ss_ref, a.shape[1])
    ug = jnp.dot(a, wg_ref[...], preferred_element_type=F32) * rstd
    uv = jnp.dot(a, wv_ref[...], preferred_element_type=F32) * rstd
    yg, tail_g = _causal_conv(ug, carry_ref[j, 0], cwg_ref[...], bg_ref[...])
    yv, tail_v = _causal_conv(uv, carry_ref[j, 1], cwv_ref[...], bv_ref[...])
    carry_ref[j, 0] = tail_g
    carry_ref[j, 1] = tail_v
    o_ref[...] = (yg * jax.nn.sigmoid(yg) * yv).astype(BF16)


def _ffn_in(xb, ss, w_in, conv_w, conv_b):
    s, d = xb.shape
    fp = w_in.shape[1] // 2
    tm = _tile(s, 1024)
    tn = _tile(fp, 512)
    nj = fp // tn
    gate = lambda i, j: (0, j)
    val = lambda i, j: (0, j + nj)
    return pl.pallas_call(
        _ffn_in_kernel,
        grid=(s // tm, nj),
        in_specs=[
            pl.BlockSpec((tm, d), lambda i, j: (i, 0)),
            pl.BlockSpec((tm, LANE), lambda i, j: (i, 0)),
            pl.BlockSpec((d, tn), gate),
            pl.BlockSpec((d, tn), val),
            pl.BlockSpec((CONV_WIDTH, tn), gate),
            pl.BlockSpec((CONV_WIDTH, tn), val),
            pl.BlockSpec((1, tn), gate),
            pl.BlockSpec((1, tn), val),
        ],
        out_specs=pl.BlockSpec((tm, tn), lambda i, j: (i, j)),
        out_shape=jax.ShapeDtypeStruct((s, fp), BF16),
        scratch_shapes=[pltpu.VMEM((nj, 2, CARRY_ROWS, tn), F32)],
        compiler_params=_params("arbitrary", "arbitrary"),
        name="ffn_in_conv_act",
    )(xb, ss, w_in, w_in, conv_w, conv_w, conv_b, conv_b)


def _mm_res_kernel(a_ref, w_ref, r_ref, o_ref, ob_ref, ss_ref, acc_ref, *, nk):
    j = pl.program_id(1)
    k = pl.program_id(2)
    prod = jnp.dot(a_ref[...], w_ref[...], preferred_element_type=F32)
    if nk == 1:
        _emit_stream(r_ref[...] + prod, j, o_ref, ob_ref, ss_ref)
        return

    @pl.when(k == 0)
    def _():
        acc_ref[...] = prod

    @pl.when(jnp.logical_and(k > 0, k < nk - 1))
    def _():
        acc_ref[...] += prod

    @pl.when(k == nk - 1)
    def _():
        _emit_stream(r_ref[...] + (acc_ref[...] + prod), j, o_ref, ob_ref, ss_ref)


def _mm_res(a, w, res, tn_want, tk_want):
    s, kdim = a.shape
    n = w.shape[1]
    tm = _tile(s, 1024)
    tn = _tile(n, tn_want)
    tk = _tile(kdim, tk_want)
    nk = kdim // tk
    tile = pl.BlockSpec((tm, tn), lambda i, j, k: (i, j))
    return pl.pallas_call(
        functools.partial(_mm_res_kernel, nk=nk),
        grid=(s // tm, n // tn, nk),
        in_specs=[
            pl.BlockSpec((tm, tk), lambda i, j, k: (i, k)),
            pl.BlockSpec((tk, tn), lambda i, j, k: (k, j)),
            tile,
        ],
        out_specs=[tile, tile, pl.BlockSpec((tm, LANE), lambda i, j, k: (i, 0))],
        out_shape=_stream_shapes(s, n),
        scratch_shapes=[pltpu.VMEM((tm, tn), F32)],
        compiler_params=_params("arbitrary", "arbitrary", "arbitrary"),
        name="matmul_residual",
    )(a, w, res)


def _ple_kernel(a_ref, ss_ref, wg_ref, p_ref, wp_ref, x_ref, o_ref, ob_ref, sso_ref):
    a = a_ref[...]
    z = jnp.dot(a, wg_ref[...], preferred_element_type=F32) * _rstd(ss_ref, a.shape[1])
    e = jnp.dot(p_ref[...].astype(BF16), wp_ref[...], preferred_element_type=F32)
    _emit_stream(x_ref[...] + jax.nn.sigmoid(z) * e, pl.program_id(1), o_ref, ob_ref, sso_ref)


def _ple(xb, ss, w_gate, p, w_ple, x):
    s, d = x.shape
    pd = p.shape[1]
    tm = _tile(s, 1024)
    tn = _tile(d, 512)
    tile = pl.BlockSpec((tm, tn), lambda i, j: (i, j))
    stats = pl.BlockSpec((tm, LANE), lambda i, j: (i, 0))
    return pl.pallas_call(
        _ple_kernel,
        grid=(s // tm, d // tn),
        in_specs=[
            pl.BlockSpec((tm, d), lambda i, j: (i, 0)),
            stats,
            pl.BlockSpec((d, tn), lambda i, j: (0, j)),
            pl.BlockSpec((tm, pd), lambda i, j: (i, 0)),
            pl.BlockSpec((pd, tn), lambda i, j: (0, j)),
            tile,
        ],
        out_specs=[tile, tile, stats],
        out_shape=_stream_shapes(s, d),
        compiler_params=_params("arbitrary", "arbitrary"),
        name="per_layer_embed",
    )(xb, ss, w_gate, p, w_ple, x)


def _proj_kernel(a_ref, ss_ref, w_ref, g_ref, o_ref, *, n_norm_tiles, out_scale):
    j = pl.program_id(1)
    a = a_ref[...]
    z = jnp.dot(a, w_ref[...], preferred_element_type=F32) * _rstd(ss_ref, a.shape[1])
    tn = z.shape[1]

    @pl.when(j < n_norm_tiles)
    def _():
        g = g_ref[...] * out_scale
        for hh in range(tn // HEAD_DIM):
            sl = slice(hh * HEAD_DIM, (hh + 1) * HEAD_DIM)
            zh = z[:, sl]
            ms = jnp.mean(zh * zh, axis=-1, keepdims=True)
            o_ref[:, sl] = (zh * lax.rsqrt(ms + EPS) * g).astype(BF16)

    @pl.when(j >= n_norm_tiles)
    def _():
        o_ref[...] = z.astype(BF16)


def _proj_heads(xb, ss, w, head_gain, n_norm_cols, out_scale):
    s, d = xb.shape
    n = w.shape[1]
    tm = _tile(s, 1024)
    tn = _tile(n, 512)
    return pl.pallas_call(
        functools.partial(_proj_kernel, n_norm_tiles=n_norm_cols // tn, out_scale=out_scale),
        grid=(s // tm, n // tn),
        in_specs=[
            pl.BlockSpec((tm, d), lambda i, j: (i, 0)),
            pl.BlockSpec((tm, LANE), lambda i, j: (i, 0)),
            pl.BlockSpec((d, tn), lambda i, j: (0, j)),
            pl.BlockSpec((1, HEAD_DIM), lambda i, j: (0, 0)),
        ],
        out_specs=pl.BlockSpec((tm, tn), lambda i, j: (i, j)),
        out_shape=jax.ShapeDtypeStruct((s, n), BF16),
        compiler_params=_params("arbitrary", "arbitrary"),
        name="head_projection",
    )(xb, ss, w, head_gain)


def _gate_kernel(a_ref, ss_ref, w_ref, b_ref, o_ref, carry_ref, *, tm):
    i = pl.program_id(0)

    @pl.when(i == 0)
    def _():
        carry_ref[...] = jnp.zeros(carry_ref.shape, F32)

    a = a_ref[...]
    z = jnp.dot(a, w_ref[...], preferred_element_type=F32) * _rstd(ss_ref, a.shape[1]) + b_ref[...]
    run = jnp.minimum(z, 0.0) - jnp.log1p(jnp.exp(-jnp.abs(z)))
    row = lax.broadcasted_iota(jnp.int32, run.shape, 0)
    shift = 1
    while shift < tm:
        run = run + jnp.where(row >= shift, pltpu.roll(run, shift, axis=0), 0.0)
        shift *= 2
    run = run + carry_ref[...]
    o_ref[...] = run
    carry_ref[...] = run[tm - 1:tm, :]


def _forget_cumsum(xb, ss, w_f, b_f):
    s, d = xb.shape
    n = w_f.shape[1]
    tm = _tile(s, 512)
    return pl.pallas_call(
        functools.partial(_gate_kernel, tm=tm),
        grid=(s // tm,),
        in_specs=[
            pl.BlockSpec((tm, d), lambda i: (i, 0)),
            pl.BlockSpec((tm, LANE), lambda i: (i, 0)),
            pl.BlockSpec((d, n), lambda i: (0, 0)),
            pl.BlockSpec((1, n), lambda i: (0, 0)),
        ],
        out_specs=pl.BlockSpec((tm, n), lambda i: (i, 0)),
        out_shape=jax.ShapeDtypeStruct((s, n), F32),
        scratch_shapes=[pltpu.VMEM((1, n), F32)],
        compiler_params=_params("arbitrary"),
        name="forget_gate_cumsum",
    )(xb, ss, w_f, b_f)


GATE_PIECES = 3
V_ROWS = HEAD_DIM + 16
QUERY_TILE = 512


def _transposed_bf16(t):
    return t.astype(F32).T.astype(BF16)


def _attn_kernel(q_ref, qc_ref, k_ref, kc_ref, v_ref, o_ref, m_ref, acc_ref, qt_ref, vt_ref, *, tq, tk):
    i = pl.program_id(1)
    seq = v_ref.shape[0]

    @pl.when(i == 0)
    def _():
        def put(cidx, carry):
            at = pl.multiple_of(cidx * tk, tk)
            vt_ref[:HEAD_DIM, pl.ds(at, tk)] = _transposed_bf16(v_ref[pl.ds(at, tk), :])
            return carry

        lax.fori_loop(0, seq // tk, put, 0)
        extra = lax.broadcasted_iota(jnp.int32, (V_ROWS - HEAD_DIM, seq), 0)
        vt_ref[HEAD_DIM:, :] = jnp.where(extra == 0, 1.0, 0.0).astype(BF16)

    qt_ref[:HEAD_DIM, :] = _transposed_bf16(q_ref[...])
    qt_ref[HEAD_DIM:, :] = _transposed_bf16(qc_ref[0])
    m_ref[...] = jnp.full(m_ref.shape, -jnp.inf, F32)
    acc_ref[...] = jnp.zeros(acc_ref.shape, F32)

    def keys(start, rows):
        return jnp.concatenate([k_ref[pl.ds(start, rows), :], kc_ref[0, pl.ds(start, rows), :]], axis=1)

    def scores(n, k_rows):
        cols = slice(n * QUERY_TILE, (n + 1) * QUERY_TILE)
        return jnp.dot(k_rows, qt_ref[:, cols], preferred_element_type=F32)

    def attend(n, st, vt_cols, diag_off):
        cols = slice(n * QUERY_TILE, (n + 1) * QUERY_TILE)
        if diag_off is not None:
            key = lax.broadcasted_iota(jnp.int32, st.shape, 0)
            qry = lax.broadcasted_iota(jnp.int32, st.shape, 1) + diag_off
            st = jnp.where(key <= qry, st, -jnp.inf)
        m_old = m_ref[:, cols]
        m_new = jnp.maximum(m_old, jnp.max(st, axis=0, keepdims=True))
        pt = jnp.exp2(st - m_new).astype(BF16)
        alpha = jnp.exp2(m_old - m_new)
        acc_ref[:, cols] = alpha * acc_ref[:, cols] + jnp.dot(vt_cols, pt, preferred_element_type=F32)
        m_ref[:, cols] = m_new

    def body(j, carry):
        start = pl.multiple_of(j * tk, tk)
        k_rows = keys(start, tk)
        vt_cols = vt_ref[:, pl.ds(start, tk)]
        st = scores(0, k_rows)
        for n in range(nt):
            st_next = scores(n + 1, k_rows) if n + 1 < nt else None
            attend(n, st, vt_cols, None)
            st = st_next
        return carry

    nt = tq // QUERY_TILE
    lax.fori_loop(0, (i * tq) // tk, body, 0)
    base = pl.multiple_of(i * tq, tq)
    diag_keys = lambda n: keys(base, (n + 1) * QUERY_TILE)
    st = scores(0, diag_keys(0))
    for n in range(nt):
        st_next = scores(n + 1, diag_keys(n + 1)) if n + 1 < nt else None
        attend(n, st, vt_ref[:, pl.ds(base, (n + 1) * QUERY_TILE)], n * QUERY_TILE)
        st = st_next
    inv_l = 1.0 / acc_ref[HEAD_DIM:HEAD_DIM + 1, :]
    o_ref[...] = (acc_ref[:HEAD_DIM, :] * inv_l).T.astype(BF16)


def _split_bf16(t):
    pieces = []
    for _ in range(GATE_PIECES):
        top = lax.bitcast_convert_type(
            lax.bitcast_convert_type(t, jnp.uint32) & jnp.uint32(0xFFFF0000), F32)
        pieces.append(top.astype(BF16))
        t = t - top
    return pieces


def _gate_columns(c, gate_first):
    cp = jnp.stack(_split_bf16(c * LOG2E), axis=-1)
    const = jnp.full(cp.shape, 1.0 if gate_first else -1.0, BF16)
    cols = jnp.concatenate([cp, const] if gate_first else [const, cp], axis=-1)
    cols = jnp.pad(cols, ((0, 0), (0, 0), (0, HEAD_DIM - 2 * GATE_PIECES)))
    return cols.transpose(1, 0, 2)


def _attention(q, kv, c):
    s, d = q.shape
    nh = d // HEAD_DIM
    tq = _tile(s, 2048)
    tk = _tile(tq, 1024)
    assert tq % QUERY_TILE == 0
    return pl.pallas_call(
        functools.partial(_attn_kernel, tq=tq, tk=tk),
        grid=(nh, s // tq),
        in_specs=[
            pl.BlockSpec((tq, HEAD_DIM), lambda h, i: (i, h)),
            pl.BlockSpec((1, tq, HEAD_DIM), lambda h, i: (h, i, 0)),
            pl.BlockSpec((s, HEAD_DIM), lambda h, i: (0, h)),
            pl.BlockSpec((1, s, HEAD_DIM), lambda h, i: (h, 0, 0)),
            pl.BlockSpec((s, HEAD_DIM), lambda h, i: (0, nh + h)),
        ],
        out_specs=pl.BlockSpec((tq, HEAD_DIM), lambda h, i: (i, h)),
        out_shape=jax.ShapeDtypeStruct((s, d), BF16),
        scratch_shapes=[
            pltpu.VMEM((1, tq), F32),
            pltpu.VMEM((V_ROWS, tq), F32),
            pltpu.VMEM((2 * HEAD_DIM, tq), BF16),
            pltpu.VMEM((V_ROWS, s), BF16),
        ],
        compiler_params=_params("arbitrary", "arbitrary"),
        name="forgetting_attention",
    )(q, _gate_columns(c, False), kv, _gate_columns(c, True), kv)


def _pad_ff(t, f, fp, axis):
    pad = [(0, 0)] * t.ndim
    pad[axis] = (0, fp - f)
    return jnp.pad(t, pad)


def _split_pad(t, f, fp):
    return jnp.concatenate([_pad_ff(t[..., :f], f, fp, -1), _pad_ff(t[..., f:], f, fp, -1)], axis=-1)


def _gained(g, w):
    return (g[:, None] * w).astype(BF16)


def _conv_ffn(x, xb, ss, g, w_in, conv_w, conv_b, w_out):
    f = w_out.shape[0]
    fp = -(-f // FF_ALIGN) * FF_ALIGN
    act = _ffn_in(xb, ss, _split_pad(_gained(g, w_in), f, fp), _split_pad(conv_w, f, fp),
                  _split_pad(conv_b[None, :], f, fp))
    return _mm_res(act, _pad_ff(w_out, f, fp, 0).astype(BF16), x, 1024, fp // 4)


def kernel(x, p, pool_norm, w_pool, pool_scale, attn_norm, w_q, q_norm, w_o, kv_norm, w_kvf, b_f,
           k_norm, ffn_norm, w_in, conv_w, conv_b, w_out, w_ple, gate_norm, w_gate):
    b, s, d = x.shape
    assert b == 1 and d % (HEAD_DIM * len(POOL_WINDOWS)) == 0
    depth = p.shape[0]
    n_a = pool_norm.shape[0]
    nh = d // HEAD_DIM
    assert nh <= LANE
    x = x[0]
    row = lambda t: t[None, :]
    kv = c_heads = xb = ss = None
    for i in range(depth):
        if i < n_a:
            x, xb, ss = _pool_layer(x, row(pool_norm[i]), w_pool[i].astype(BF16), row(pool_scale[i]))
        else:
            j = i - n_a
            q = _proj_heads(xb, ss, _gained(attn_norm[j], w_q[j]), row(q_norm[j]), d,
                            HEAD_DIM ** -0.5 * LOG2E)
            att = _attention(q, kv, c_heads)
            x, xb, ss = _mm_res(att, w_o[j].astype(BF16), x, 512, d)
        x, xb, ss = _conv_ffn(x, xb, ss, ffn_norm[i], w_in[i], conv_w[i], conv_b[i], w_out[i])
        x, xb, ss = _ple(xb, ss, _gained(gate_norm[i], w_gate[i]), p[i, 0], w_ple[i].astype(BF16), x)
        if i == n_a - 1:
            w_kvf_g = _gained(kv_norm, w_kvf)
            kv = _proj_heads(xb, ss, w_kvf_g[:, :2 * d], row(k_norm), d, 1.0)
            w_f = jnp.pad(w_kvf_g[:, 2 * d:], ((0, 0), (0, LANE - nh)))
            c = _forget_cumsum(xb, ss, w_f, jnp.pad(b_f, (0, LANE - nh))[None, :])
            c_heads = c[:, :nh]
    return x[None]
```

```python
import functools
import math

import jax
import jax.numpy as jnp
from jax import lax
from jax.experimental import pallas as pl
from jax.experimental.pallas import tpu as pltpu

HEAD_DIM = 128
POOL_WINDOWS = (2, 4, 8, 16)
POOL_HALO = 16
CONV_WIDTH = 3
CARRY_ROWS = 8
EPS = 1e-6
LOG2E = math.log2(math.e)
LANE = 128
FF_ALIGN = 1024
VMEM_LIMIT = 56 * 1024 * 1024

F32 = jnp.float32
BF16 = jnp.bfloat16


def _tile(n, want):
    t = min(n, want)
    while n % t:
        t //= 2
    return t


def _params(*sem):
    return pltpu.CompilerParams(dimension_semantics=sem, vmem_limit_bytes=VMEM_LIMIT)


def _rms(x, g):
    ms = jnp.mean(x * x, axis=-1, keepdims=True)
    return x * lax.rsqrt(ms + EPS) * g


def _rstd(ss_ref, d):
    return lax.rsqrt(ss_ref[:, :1] * (1.0 / d) + EPS)


def _emit_stream(out, j, o_ref, ob_ref, ss_ref):
    o_ref[...] = out
    ob_ref[...] = out.astype(BF16)
    part = jnp.broadcast_to(jnp.sum(out * out, axis=1, keepdims=True), ss_ref.shape)

    @pl.when(j == 0)
    def _():
        ss_ref[...] = part

    @pl.when(j > 0)
    def _():
        ss_ref[...] += part


def _stream_shapes(s, d):
    return [jax.ShapeDtypeStruct((s, d), F32), jax.ShapeDtypeStruct((s, d), BF16),
            jax.ShapeDtypeStruct((s, LANE), F32)]


def _pool_kernel(x_ref, halo_ref, g_ref, w_ref, sc_ref, xo_ref, xb_ref, ss_ref, *, tm):
    i = pl.program_id(0)
    x = x_ref[...]
    g = g_ref[...]
    h = _rms(x, g)
    hh = jnp.where(i > 0, _rms(halo_ref[...], g), 0.0)
    cur = jnp.concatenate([hh, h], axis=0)
    d = x.shape[1]
    c = d // len(POOL_WINDOWS)
    pos = i * tm + lax.broadcasted_iota(jnp.int32, (tm, 1), 0)
    outs = []
    for gi, w in enumerate(POOL_WINDOWS):
        cur = cur + pltpu.roll(cur, w // 2, axis=0)
        win = cur[POOL_HALO:, :c]
        if gi + 1 < len(POOL_WINDOWS):
            cur = cur[:, c:]
        inv_cnt = 1.0 / jnp.minimum(pos + 1, w).astype(F32)
        pooled = win * inv_cnt - h[:, gi * c:(gi + 1) * c]
        outs.append(jnp.dot(pooled.astype(BF16), w_ref[gi], preferred_element_type=F32))
    x1 = x + jnp.concatenate(outs, axis=1) * sc_ref[...]
    xo_ref[...] = x1
    xb_ref[...] = x1.astype(BF16)
    ss_ref[...] = jnp.broadcast_to(jnp.sum(x1 * x1, axis=1, keepdims=True), ss_ref.shape)


def _pool_layer(x, g, w_pool, scale):
    s, d = x.shape
    tm = _tile(s, 256)
    hb = tm // POOL_HALO
    row = lambda i: (i, 0)
    const2 = lambda i: (0, 0)
    return pl.pallas_call(
        functools.partial(_pool_kernel, tm=tm),
        grid=(s // tm,),
        in_specs=[
            pl.BlockSpec((tm, d), row),
            pl.BlockSpec((POOL_HALO, d), lambda i: (jnp.maximum(i * hb - 1, 0), 0)),
            pl.BlockSpec((1, d), const2),
            pl.BlockSpec(w_pool.shape, lambda i: (0, 0, 0)),
            pl.BlockSpec((1, d), const2),
        ],
        out_specs=[pl.BlockSpec((tm, d), row), pl.BlockSpec((tm, d), row), pl.BlockSpec((tm, LANE), row)],
        out_shape=_stream_shapes(s, d),
        compiler_params=_params("arbitrary"),
        name="pool_mixer",
    )(x, x, g, w_pool, scale)


def _causal_conv(u, prev, cw, b):
    tm = u.shape[0]
    ext = jnp.concatenate([prev, u], axis=0)
    u1 = pltpu.roll(ext, 1, axis=0)[CARRY_ROWS:]
    u2 = pltpu.roll(ext, 2, axis=0)[CARRY_ROWS:]
    return cw[2:3] * u + cw[1:2] * u1 + cw[0:1] * u2 + b, ext[tm:]


def _ffn_in_kernel(a_ref, ss_ref, wg_ref, wv_ref, cwg_ref, cwv_ref, bg_ref, bv_ref, o_ref, carry_ref):
    i = pl.program_id(0)
    j = pl.program_id(1)

    @pl.when(i == 0)
    def _():
        carry_ref[j] = jnp.zeros(carry_ref.shape[1:], F32)

    a = a_ref[...]
    rstd = _rstd(ss_ref, a.shape[1])
    ug = jnp.dot(a, wg_ref[...], preferred_element_type=F32) * rstd
    uv = jnp.dot(a, wv_ref[...], preferred_element_type=F32) * rstd
    yg, tail_g = _causal_conv(ug, carry_ref[j, 0], cwg_ref[...], bg_ref[...])
    yv, tail_v = _causal_conv(uv, carry_ref[j, 1], cwv_ref[...], bv_ref[...])
    carry_ref[j, 0] = tail_g
    carry_ref[j, 1] = tail_v
    o_ref[...] = (yg * jax.nn.sigmoid(yg) * yv).astype(BF16)


def _ffn_in(xb, ss, w_in, conv_w, conv_b):
    s, d = xb.shape
    fp = w_in.shape[1] // 2
    tm = _tile(s, 1024)
    tn = _tile(fp, 512)
    nj = fp // tn
    gate = lambda i, j: (0, j)
    val = lambda i, j: (0, j + nj)
    return pl.pallas_call(
        _ffn_in_kernel,
        grid=(s // tm, nj),
        in_specs=[
            pl.BlockSpec((tm, d), lambda i, j: (i, 0)),
            pl.BlockSpec((tm, LANE), lambda i, j: (i, 0)),
            pl.BlockSpec((d, tn), gate),
            pl.BlockSpec((d, tn), val),
            pl.BlockSpec((CONV_WIDTH, tn), gate),
            pl.BlockSpec((CONV_WIDTH, tn), val),
            pl.BlockSpec((1, tn), gate),
            pl.BlockSpec((1, tn), val),
        ],
        out_specs=pl.BlockSpec((tm, tn), lambda i, j: (i, j)),
        out_shape=jax.ShapeDtypeStruct((s, fp), BF16),
        scratch_shapes=[pltpu.VMEM((nj, 2, CARRY_ROWS, tn), F32)],
        compiler_params=_params("arbitrary", "arbitrary"),
        name="ffn_in_conv_act",
    )(xb, ss, w_in, w_in, conv_w, conv_w, conv_b, conv_b)


def _mm_res_kernel(a_ref, w_ref, r_ref, o_ref, ob_ref, ss_ref, acc_ref, *, nk):
    j = pl.program_id(1)
    k = pl.program_id(2)
    prod = jnp.dot(a_ref[...], w_ref[...], preferred_element_type=F32)
    if nk == 1:
        _emit_stream(r_ref[...] + prod, j, o_ref, ob_ref, ss_ref)
        return

    @pl.when(k == 0)
    def _():
        acc_ref[...] = prod

    @pl.when(jnp.logical_and(k > 0, k < nk - 1))
    def _():
        acc_ref[...] += prod

    @pl.when(k == nk - 1)
    def _():
        _emit_stream(r_ref[...] + (acc_ref[...] + prod), j, o_ref, ob_ref, ss_ref)


def _mm_res(a, w, res, tn_want, tk_want):
    s, kdim = a.shape
    n = w.shape[1]
    tm = _tile(s, 1024)
    tn = _tile(n, tn_want)
    tk = _tile(kdim, tk_want)
    nk = kdim // tk
    tile = pl.BlockSpec((tm, tn), lambda i, j, k: (i, j))
    return pl.pallas_call(
        functools.partial(_mm_res_kernel, nk=nk),
        grid=(s // tm, n // tn, nk),
        in_specs=[
            pl.BlockSpec((tm, tk), lambda i, j, k: (i, k)),
            pl.BlockSpec((tk, tn), lambda i, j, k: (k, j)),
            tile,
        ],
        out_specs=[tile, tile, pl.BlockSpec((tm, LANE), lambda i, j, k: (i, 0))],
        out_shape=_stream_shapes(s, n),
        scratch_shapes=[pltpu.VMEM((tm, tn), F32)],
        compiler_params=_params("arbitrary", "arbitrary", "arbitrary"),
        name="matmul_residual",
    )(a, w, res)


def _ple_kernel(a_ref, ss_ref, wg_ref, p_ref, wp_ref, x_ref, o_ref, ob_ref, sso_ref):
    a = a_ref[...]
    z = jnp.dot(a, wg_ref[...], preferred_element_type=F32) * _rstd(ss_ref, a.shape[1])
    e = jnp.dot(p_ref[...].astype(BF16), wp_ref[...], preferred_element_type=F32)
    _emit_stream(x_ref[...] + jax.nn.sigmoid(z) * e, pl.program_id(1), o_ref, ob_ref, sso_ref)


def _ple(xb, ss, w_gate, p, w_ple, x):
    s, d = x.shape
    pd = p.shape[1]
    tm = _tile(s, 1024)
    tn = _tile(d, 512)
    tile = pl.BlockSpec((tm, tn), lambda i, j: (i, j))
    stats = pl.BlockSpec((tm, LANE), lambda i, j: (i, 0))
    return pl.pallas_call(
        _ple_kernel,
        grid=(s // tm, d // tn),
        in_specs=[
            pl.BlockSpec((tm, d), lambda i, j: (i, 0)),
            stats,
            pl.BlockSpec((d, tn), lambda i, j: (0, j)),
            pl.BlockSpec((tm, pd), lambda i, j: (i, 0)),
            pl.BlockSpec((pd, tn), lambda i, j: (0, j)),
            tile,
        ],
        out_specs=[tile, tile, stats],
        out_shape=_stream_shapes(s, d),
        compiler_params=_params("arbitrary", "arbitrary"),
        name="per_layer_embed",
    )(xb, ss, w_gate, p, w_ple, x)


def _proj_kernel(a_ref, ss_ref, w_ref, g_ref, o_ref, *, n_norm_tiles, out_scale):
    j = pl.program_id(1)
    a = a_ref[...]
    z = jnp.dot(a, w_ref[...], preferred_element_type=F32) * _rstd(ss_ref, a.shape[1])
    tn = z.shape[1]

    @pl.when(j < n_norm_tiles)
    def _():
        g = g_ref[...] * out_scale
        for hh in range(tn // HEAD_DIM):
            sl = slice(hh * HEAD_DIM, (hh + 1) * HEAD_DIM)
            zh = z[:, sl]
            ms = jnp.mean(zh * zh, axis=-1, keepdims=True)
            o_ref[:, sl] = (zh * lax.rsqrt(ms + EPS) * g).astype(BF16)

    @pl.when(j >= n_norm_tiles)
    def _():
        o_ref[...] = z.astype(BF16)


def _proj_heads(xb, ss, w, head_gain, n_norm_cols, out_scale):
    s, d = xb.shape
    n = w.shape[1]
    tm = _tile(s, 1024)
    tn = _tile(n, 1024)
    return pl.pallas_call(
        functools.partial(_proj_kernel, n_norm_tiles=n_norm_cols // tn, out_scale=out_scale),
        grid=(s // tm, n // tn),
        in_specs=[
            pl.BlockSpec((tm, d), lambda i, j: (i, 0)),
            pl.BlockSpec((tm, LANE), lambda i, j: (i, 0)),
            pl.BlockSpec((d, tn), lambda i, j: (0, j)),
            pl.BlockSpec((1, HEAD_DIM), lambda i, j: (0, 0)),
        ],
        out_specs=pl.BlockSpec((tm, tn), lambda i, j: (i, j)),
        out_shape=jax.ShapeDtypeStruct((s, n), BF16),
        compiler_params=_params("arbitrary", "arbitrary"),
        name="head_projection",
    )(xb, ss, w, head_gain)


def _gate_kernel(a_ref, ss_ref, w_ref, b_ref, o_ref, carry_ref, *, tm):
    i = pl.program_id(0)

    @pl.when(i == 0)
    def _():
        carry_ref[...] = jnp.zeros(carry_ref.shape, F32)

    a = a_ref[...]
    z = jnp.dot(a, w_ref[...], preferred_element_type=F32) * _rstd(ss_ref, a.shape[1]) + b_ref[...]
    run = jnp.minimum(z, 0.0) - jnp.log1p(jnp.exp(-jnp.abs(z)))
    row = lax.broadcasted_iota(jnp.int32, run.shape, 0)
    shift = 1
    while shift < tm:
        run = run + jnp.where(row >= shift, pltpu.roll(run, shift, axis=0), 0.0)
        shift *= 2
    run = run + carry_ref[...]
    o_ref[...] = run
    carry_ref[...] = run[tm - 1:tm, :]


def _forget_cumsum(xb, ss, w_f, b_f):
    s, d = xb.shape
    n = w_f.shape[1]
    tm = _tile(s, 512)
    return pl.pallas_call(
        functools.partial(_gate_kernel, tm=tm),
        grid=(s // tm,),
        in_specs=[
            pl.BlockSpec((tm, d), lambda i: (i, 0)),
            pl.BlockSpec((tm, LANE), lambda i: (i, 0)),
            pl.BlockSpec((d, n), lambda i: (0, 0)),
            pl.BlockSpec((1, n), lambda i: (0, 0)),
        ],
        out_specs=pl.BlockSpec((tm, n), lambda i: (i, 0)),
        out_shape=jax.ShapeDtypeStruct((s, n), F32),
        scratch_shapes=[pltpu.VMEM((1, n), F32)],
        compiler_params=_params("arbitrary"),
        name="forget_gate_cumsum",
    )(xb, ss, w_f, b_f)


GATE_PIECES = 3
V_ROWS = HEAD_DIM + 16
QUERY_TILE = 512


def _transposed_bf16(t):
    return t.astype(F32).T.astype(BF16)


def _attn_kernel(q_ref, qc_ref, k_ref, kc_ref, v_ref, o_ref, m_ref, acc_ref, qt_ref, vt_ref, *, tq, tk):
    i = pl.program_id(1)
    seq = v_ref.shape[0]

    @pl.when(i == 0)
    def _():
        def put(cidx, carry):
            at = pl.multiple_of(cidx * tk, tk)
            vt_ref[:HEAD_DIM, pl.ds(at, tk)] = _transposed_bf16(v_ref[pl.ds(at, tk), :])
            return carry

        lax.fori_loop(0, seq // tk, put, 0)
        extra = lax.broadcasted_iota(jnp.int32, (V_ROWS - HEAD_DIM, seq), 0)
        vt_ref[HEAD_DIM:, :] = jnp.where(extra == 0, 1.0, 0.0).astype(BF16)

    qt_ref[:HEAD_DIM, :] = _transposed_bf16(q_ref[...])
    qt_ref[HEAD_DIM:, :] = _transposed_bf16(qc_ref[0])
    m_ref[...] = jnp.full(m_ref.shape, -jnp.inf, F32)
    acc_ref[...] = jnp.zeros(acc_ref.shape, F32)

    def keys(start, rows):
        return jnp.concatenate([k_ref[pl.ds(start, rows), :], kc_ref[0, pl.ds(start, rows), :]], axis=1)

    def scores(n, k_rows):
        cols = slice(n * QUERY_TILE, (n + 1) * QUERY_TILE)
        return jnp.dot(k_rows, qt_ref[:, cols], preferred_element_type=F32)

    def attend(n, st, vt_cols, diag_off):
        cols = slice(n * QUERY_TILE, (n + 1) * QUERY_TILE)
        if diag_off is not None:
            key = lax.broadcasted_iota(jnp.int32, st.shape, 0)
            qry = lax.broadcasted_iota(jnp.int32, st.shape, 1) + diag_off
            st = jnp.where(key <= qry, st, -jnp.inf)
        m_old = m_ref[:, cols]
        m_new = jnp.maximum(m_old, jnp.max(st, axis=0, keepdims=True))
        pt = jnp.exp2(st - m_new).astype(BF16)
        alpha = jnp.exp2(m_old - m_new)
        acc_ref[:, cols] = alpha * acc_ref[:, cols] + jnp.dot(vt_cols, pt, preferred_element_type=F32)
        m_ref[:, cols] = m_new

    def run(units):
        loaded = {}

        def block_keys(start):
            if id(start) not in loaded:
                loaded[id(start)] = keys(start, tk)
            return loaded[id(start)]

        st = scores(units[0][0], block_keys(units[0][1]))
        st_next = None
        for u, (n, start, on_diag) in enumerate(units):
            if u + 1 < len(units):
                st_next = scores(units[u + 1][0], block_keys(units[u + 1][1]))
            attend(n, st, vt_ref[:, pl.ds(start, tk)], 0 if on_diag else None)
            st = st_next

    def body(jj, carry):
        starts = [pl.multiple_of((jj * blocks_per_trip + b) * tk, tk) for b in range(blocks_per_trip)]
        run([(n, start, False) for start in starts for n in range(nt)])
        return carry

    nt = tq // QUERY_TILE
    blocks_per_trip = tq // tk
    lax.fori_loop(0, i, body, 0)
    base = pl.multiple_of(i * tq, tq)
    diag_starts = [base + kb * tk for kb in range(nt)]
    run([(n, diag_starts[kb], kb == n) for kb in range(nt) for n in range(kb, nt)])
    inv_l = 1.0 / acc_ref[HEAD_DIM:HEAD_DIM + 1, :]
    o_ref[...] = (acc_ref[:HEAD_DIM, :] * inv_l).T.astype(BF16)


def _split_bf16(t):
    pieces = []
    for _ in range(GATE_PIECES):
        top = lax.bitcast_convert_type(
            lax.bitcast_convert_type(t, jnp.uint32) & jnp.uint32(0xFFFF0000), F32)
        pieces.append(top.astype(BF16))
        t = t - top
    return pieces


def _gate_columns(c, gate_first):
    cp = jnp.stack(_split_bf16(c * LOG2E), axis=-1)
    const = jnp.full(cp.shape, 1.0 if gate_first else -1.0, BF16)
    cols = jnp.concatenate([cp, const] if gate_first else [const, cp], axis=-1)
    cols = jnp.pad(cols, ((0, 0), (0, 0), (0, HEAD_DIM - 2 * GATE_PIECES)))
    return cols.transpose(1, 0, 2)


def _attention(q, kv, c):
    s, d = q.shape
    nh = d // HEAD_DIM
    tq = _tile(s, 2048)
    tk = QUERY_TILE
    assert tq % QUERY_TILE == 0
    return pl.pallas_call(
        functools.partial(_attn_kernel, tq=tq, tk=tk),
        grid=(nh, s // tq),
        in_specs=[
            pl.BlockSpec((tq, HEAD_DIM), lambda h, i: (i, h)),
            pl.BlockSpec((1, tq, HEAD_DIM), lambda h, i: (h, i, 0)),
            pl.BlockSpec((s, HEAD_DIM), lambda h, i: (0, h)),
            pl.BlockSpec((1, s, HEAD_DIM), lambda h, i: (h, 0, 0)),
            pl.BlockSpec((s, HEAD_DIM), lambda h, i: (0, nh + h)),
        ],
        out_specs=pl.BlockSpec((tq, HEAD_DIM), lambda h, i: (i, h)),
        out_shape=jax.ShapeDtypeStruct((s, d), BF16),
        scratch_shapes=[
            pltpu.VMEM((1, tq), F32),
            pltpu.VMEM((V_ROWS, tq), F32),
            pltpu.VMEM((2 * HEAD_DIM, tq), BF16),
            pltpu.VMEM((V_ROWS, s), BF16),
        ],
        compiler_params=_params("arbitrary", "arbitrary"),
        name="forgetting_attention",
    )(q, _gate_columns(c, False), kv, _gate_columns(c, True), kv)


def _pad_ff(t, f, fp, axis):
    pad = [(0, 0)] * t.ndim
    pad[axis] = (0, fp - f)
    return jnp.pad(t, pad)


def _split_pad(t, f, fp):
    return jnp.concatenate([_pad_ff(t[..., :f], f, fp, -1), _pad_ff(t[..., f:], f, fp, -1)], axis=-1)


def _gained(g, w):
    return (g[:, None] * w).astype(BF16)


def _conv_ffn(x, xb, ss, g, w_in, conv_w, conv_b, w_out):
    f = w_out.shape[0]
    fp = -(-f // FF_ALIGN) * FF_ALIGN
    act = _ffn_in(xb, ss, _split_pad(_gained(g, w_in), f, fp), _split_pad(conv_w, f, fp),
                  _split_pad(conv_b[None, :], f, fp))
    return _mm_res(act, _pad_ff(w_out, f, fp, 0).astype(BF16), x, 1024, fp // 4)


def kernel(x, p, pool_norm, w_pool, pool_scale, attn_norm, w_q, q_norm, w_o, kv_norm, w_kvf, b_f,
           k_norm, ffn_norm, w_in, conv_w, conv_b, w_out, w_ple, gate_norm, w_gate):
    b, s, d = x.shape
    assert b == 1 and d % (HEAD_DIM * len(POOL_WINDOWS)) == 0
    depth = p.shape[0]
    n_a = pool_norm.shape[0]
    nh = d // HEAD_DIM
    assert nh <= LANE
    x = x[0]
    row = lambda t: t[None, :]
    kv = c_heads = xb = ss = None
    for i in range(depth):
        if i < n_a:
            x, xb, ss = _pool_layer(x, row(pool_norm[i]), w_pool[i].astype(BF16), row(pool_scale[i]))
        else:
            j = i - n_a
            q = _proj_heads(xb, ss, _gained(attn_norm[j], w_q[j]), row(q_norm[j]), d,
                            HEAD_DIM ** -0.5 * LOG2E)
            att = _attention(q, kv, c_heads)
            x, xb, ss = _mm_res(att, w_o[j].astype(BF16), x, 512, d)
        x, xb, ss = _conv_ffn(x, xb, ss, ffn_norm[i], w_in[i], conv_w[i], conv_b[i], w_out[i])
        x, xb, ss = _ple(xb, ss, _gained(gate_norm[i], w_gate[i]), p[i, 0], w_ple[i].astype(BF16), x)
        if i == n_a - 1:
            w_kvf_g = _gained(kv_norm, w_kvf)
            kv = _proj_heads(xb, ss, w_kvf_g[:, :2 * d], row(k_norm), d, 1.0)
            w_f = jnp.pad(w_kvf_g[:, 2 * d:], ((0, 0), (0, LANE - nh)))
            c = _forget_cumsum(xb, ss, w_f, jnp.pad(b_f, (0, LANE - nh))[None, :])
            c_heads = c[:, :nh]
    return x[None]
```

```python
import functools
import math

import jax
import jax.numpy as jnp
from jax import lax
from jax.experimental import pallas as pl
from jax.experimental.pallas import tpu as pltpu

HEAD_DIM = 128
POOL_WINDOWS = (2, 4, 8, 16)
POOL_HALO = 16
CONV_WIDTH = 3
CARRY_ROWS = 8
EPS = 1e-6
LOG2E = math.log2(math.e)
LANE = 128
FF_ALIGN = 1024
VMEM_LIMIT = 56 * 1024 * 1024

F32 = jnp.float32
BF16 = jnp.bfloat16


def _tile(n, want):
    t = min(n, want)
    while n % t:
        t //= 2
    return t


def _params(*sem):
    return pltpu.CompilerParams(dimension_semantics=sem, vmem_limit_bytes=VMEM_LIMIT)


def _rms(x, g):
    ms = jnp.mean(x * x, axis=-1, keepdims=True)
    return x * lax.rsqrt(ms + EPS) * g


def _rstd(ss_ref, d):
    return lax.rsqrt(ss_ref[:, :1] * (1.0 / d) + EPS)


def _emit_stream(out, j, o_ref, ob_ref, ss_ref):
    o_ref[...] = out
    ob_ref[...] = out.astype(BF16)
    part = jnp.broadcast_to(jnp.sum(out * out, axis=1, keepdims=True), ss_ref.shape)

    @pl.when(j == 0)
    def _():
        ss_ref[...] = part

    @pl.when(j > 0)
    def _():
        ss_ref[...] += part


def _stream_shapes(s, d):
    return [jax.ShapeDtypeStruct((s, d), F32), jax.ShapeDtypeStruct((s, d), BF16),
            jax.ShapeDtypeStruct((s, LANE), F32)]


def _pool_kernel(x_ref, halo_ref, g_ref, w_ref, sc_ref, xo_ref, xb_ref, ss_ref, *, tm):
    i = pl.program_id(0)
    x = x_ref[...]
    g = g_ref[...]
    h = _rms(x, g)
    hh = jnp.where(i > 0, _rms(halo_ref[...], g), 0.0)
    cur = jnp.concatenate([hh, h], axis=0)
    d = x.shape[1]
    c = d // len(POOL_WINDOWS)
    pos = i * tm + lax.broadcasted_iota(jnp.int32, (tm, 1), 0)
    outs = []
    for gi, w in enumerate(POOL_WINDOWS):
        cur = cur + pltpu.roll(cur, w // 2, axis=0)
        win = cur[POOL_HALO:, :c]
        if gi + 1 < len(POOL_WINDOWS):
            cur = cur[:, c:]
        inv_cnt = 1.0 / jnp.minimum(pos + 1, w).astype(F32)
        pooled = win * inv_cnt - h[:, gi * c:(gi + 1) * c]
        outs.append(jnp.dot(pooled.astype(BF16), w_ref[gi], preferred_element_type=F32))
    x1 = x + jnp.concatenate(outs, axis=1) * sc_ref[...]
    xo_ref[...] = x1
    xb_ref[...] = x1.astype(BF16)
    ss_ref[...] = jnp.broadcast_to(jnp.sum(x1 * x1, axis=1, keepdims=True), ss_ref.shape)


def _pool_layer(x, g, w_pool, scale):
    s, d = x.shape
    tm = _tile(s, 256)
    hb = tm // POOL_HALO
    row = lambda i: (i, 0)
    const2 = lambda i: (0, 0)
    return pl.pallas_call(
        functools.partial(_pool_kernel, tm=tm),
        grid=(s // tm,),
        in_specs=[
            pl.BlockSpec((tm, d), row),
            pl.BlockSpec((POOL_HALO, d), lambda i: (jnp.maximum(i * hb - 1, 0), 0)),
            pl.BlockSpec((1, d), const2),
            pl.BlockSpec(w_pool.shape, lambda i: (0, 0, 0)),
            pl.BlockSpec((1, d), const2),
        ],
        out_specs=[pl.BlockSpec((tm, d), row), pl.BlockSpec((tm, d), row), pl.BlockSpec((tm, LANE), row)],
        out_shape=_stream_shapes(s, d),
        compiler_params=_params("arbitrary"),
        name="pool_mixer",
    )(x, x, g, w_pool, scale)


def _causal_conv(u, prev, cw, b):
    tm = u.shape[0]
    ext = jnp.concatenate([prev, u], axis=0)
    u1 = pltpu.roll(ext, 1, axis=0)[CARRY_ROWS:]
    u2 = pltpu.roll(ext, 2, axis=0)[CARRY_ROWS:]
    return cw[2:3] * u + cw[1:2] * u1 + cw[0:1] * u2 + b, ext[tm:]


def _ffn_in_kernel(a_ref, ss_ref, wg_ref, wv_ref, cwg_ref, cwv_ref, bg_ref, bv_ref, o_ref, carry_ref):
    i = pl.program_id(0)
    j = pl.program_id(1)

    @pl.when(i == 0)
    def _():
        carry_ref[j] = jnp.zeros(carry_ref.shape[1:], F32)

    a = a_ref[...]
    rstd = _rstd(ss_ref, a.shape[1])
    ug = jnp.dot(a, wg_ref[...], preferred_element_type=F32) * rstd
    uv = jnp.dot(a, wv_ref[...], preferred_element_type=F32) * rstd
    yg, tail_g = _causal_conv(ug, carry_ref[j, 0], cwg_ref[...], bg_ref[...])
    yv, tail_v = _causal_conv(uv, carry_ref[j, 1], cwv_ref[...], bv_ref[...])
    carry_ref[j, 0] = tail_g
    carry_ref[j, 1] = tail_v
    o_ref[...] = (yg * jax.nn.sigmoid(yg) * yv).astype(BF16)


def _ffn_in(xb, ss, w_in, conv_w, conv_b, layer):
    s, d = xb.shape
    fp = w_in.shape[2] // 2
    tm = _tile(s, 1024)
    tn = _tile(fp, 512)
    nj = fp // tn
    gate = lambda i, j: (layer, 0, j)
    val = lambda i, j: (layer, 0, j + nj)
    return pl.pallas_call(
        _ffn_in_kernel,
        grid=(s // tm, nj),
        in_specs=[
            pl.BlockSpec((tm, d), lambda i, j: (i, 0)),
            pl.BlockSpec((tm, LANE), lambda i, j: (i, 0)),
            pl.BlockSpec((None, d, tn), gate),
            pl.BlockSpec((None, d, tn), val),
            pl.BlockSpec((None, CONV_WIDTH, tn), gate),
            pl.BlockSpec((None, CONV_WIDTH, tn), val),
            pl.BlockSpec((None, 1, tn), gate),
            pl.BlockSpec((None, 1, tn), val),
        ],
        out_specs=pl.BlockSpec((tm, tn), lambda i, j: (i, j)),
        out_shape=jax.ShapeDtypeStruct((s, fp), BF16),
        scratch_shapes=[pltpu.VMEM((nj, 2, CARRY_ROWS, tn), F32)],
        compiler_params=_params("arbitrary", "arbitrary"),
        name="ffn_in_conv_act",
    )(xb, ss, w_in, w_in, conv_w, conv_w, conv_b, conv_b)


def _mm_res_kernel(a_ref, w_ref, r_ref, o_ref, ob_ref, ss_ref, acc_ref, *, nk):
    j = pl.program_id(1)
    k = pl.program_id(2)
    prod = jnp.dot(a_ref[...], w_ref[...], preferred_element_type=F32)
    if nk == 1:
        _emit_stream(r_ref[...] + prod, j, o_ref, ob_ref, ss_ref)
        return

    @pl.when(k == 0)
    def _():
        acc_ref[...] = prod

    @pl.when(jnp.logical_and(k > 0, k < nk - 1))
    def _():
        acc_ref[...] += prod

    @pl.when(k == nk - 1)
    def _():
        _emit_stream(r_ref[...] + (acc_ref[...] + prod), j, o_ref, ob_ref, ss_ref)


def _mm_res(a, w, layer, res, tn_want, tk_want):
    s, kdim = a.shape
    n = w.shape[2]
    tm = _tile(s, 1024)
    tn = _tile(n, tn_want)
    tk = _tile(kdim, tk_want)
    nk = kdim // tk
    tile = pl.BlockSpec((tm, tn), lambda i, j, k: (i, j))
    return pl.pallas_call(
        functools.partial(_mm_res_kernel, nk=nk),
        grid=(s // tm, n // tn, nk),
        in_specs=[
            pl.BlockSpec((tm, tk), lambda i, j, k: (i, k)),
            pl.BlockSpec((None, tk, tn), lambda i, j, k: (layer, k, j)),
            tile,
        ],
        out_specs=[tile, tile, pl.BlockSpec((tm, LANE), lambda i, j, k: (i, 0))],
        out_shape=_stream_shapes(s, n),
        scratch_shapes=[pltpu.VMEM((tm, tn), F32)],
        compiler_params=_params("arbitrary", "arbitrary", "arbitrary"),
        name="matmul_residual",
    )(a, w, res)


def _ple_kernel(a_ref, ss_ref, wg_ref, p_ref, wp_ref, x_ref, o_ref, ob_ref, sso_ref):
    a = a_ref[...]
    z = jnp.dot(a, wg_ref[...], preferred_element_type=F32) * _rstd(ss_ref, a.shape[1])
    e = jnp.dot(p_ref[...].astype(BF16), wp_ref[...], preferred_element_type=F32)
    _emit_stream(x_ref[...] + jax.nn.sigmoid(z) * e, pl.program_id(1), o_ref, ob_ref, sso_ref)


def _ple(xb, ss, w_gate, p, w_ple, layer, x):
    s, d = x.shape
    pd = p.shape[3]
    tm = _tile(s, 1024)
    tn = _tile(d, 512)
    tile = pl.BlockSpec((tm, tn), lambda i, j: (i, j))
    stats = pl.BlockSpec((tm, LANE), lambda i, j: (i, 0))
    return pl.pallas_call(
        _ple_kernel,
        grid=(s // tm, d // tn),
        in_specs=[
            pl.BlockSpec((tm, d), lambda i, j: (i, 0)),
            stats,
            pl.BlockSpec((None, d, tn), lambda i, j: (layer, 0, j)),
            pl.BlockSpec((None, None, tm, pd), lambda i, j: (layer, 0, i, 0)),
            pl.BlockSpec((None, pd, tn), lambda i, j: (layer, 0, j)),
            tile,
        ],
        out_specs=[tile, tile, stats],
        out_shape=_stream_shapes(s, d),
        compiler_params=_params("arbitrary", "arbitrary"),
        name="per_layer_embed",
    )(xb, ss, w_gate, p, w_ple, x)


def _proj_kernel(a_ref, ss_ref, w_ref, g_ref, o_ref, *, n_norm_tiles, out_scale):
    j = pl.program_id(1)
    a = a_ref[...]
    z = jnp.dot(a, w_ref[...], preferred_element_type=F32) * _rstd(ss_ref, a.shape[1])
    tn = z.shape[1]

    @pl.when(j < n_norm_tiles)
    def _():
        g = g_ref[...] * out_scale
        for hh in range(tn // HEAD_DIM):
            sl = slice(hh * HEAD_DIM, (hh + 1) * HEAD_DIM)
            zh = z[:, sl]
            ms = jnp.mean(zh * zh, axis=-1, keepdims=True)
            o_ref[:, sl] = (zh * lax.rsqrt(ms + EPS) * g).astype(BF16)

    @pl.when(j >= n_norm_tiles)
    def _():
        o_ref[...] = z.astype(BF16)


def _proj_heads(xb, ss, w, head_gain, n, n_norm_cols, out_scale):
    s, d = xb.shape
    tm = _tile(s, 1024)
    tn = _tile(math.gcd(n, n_norm_cols), 1024)
    assert tn % HEAD_DIM == 0
    return pl.pallas_call(
        functools.partial(_proj_kernel, n_norm_tiles=n_norm_cols // tn, out_scale=out_scale),
        grid=(s // tm, n // tn),
        in_specs=[
            pl.BlockSpec((tm, d), lambda i, j: (i, 0)),
            pl.BlockSpec((tm, LANE), lambda i, j: (i, 0)),
            pl.BlockSpec((d, tn), lambda i, j: (0, j)),
            pl.BlockSpec((1, HEAD_DIM), lambda i, j: (0, 0)),
        ],
        out_specs=pl.BlockSpec((tm, tn), lambda i, j: (i, j)),
        out_shape=jax.ShapeDtypeStruct((s, n), BF16),
        compiler_params=_params("arbitrary", "arbitrary"),
        name="head_projection",
    )(xb, ss, w, head_gain)


def _gate_kernel(a_ref, ss_ref, w_ref, b_ref, o_ref, carry_ref, *, tm):
    i = pl.program_id(0)

    @pl.when(i == 0)
    def _():
        carry_ref[...] = jnp.zeros(carry_ref.shape, F32)

    a = a_ref[...]
    z = jnp.dot(a, w_ref[...], preferred_element_type=F32) * _rstd(ss_ref, a.shape[1]) + b_ref[...]
    run = jnp.minimum(z, 0.0) - jnp.log1p(jnp.exp(-jnp.abs(z)))
    row = lax.broadcasted_iota(jnp.int32, run.shape, 0)
    shift = 1
    while shift < tm:
        run = run + jnp.where(row >= shift, pltpu.roll(run, shift, axis=0), 0.0)
        shift *= 2
    run = run + carry_ref[...]
    o_ref[...] = run
    carry_ref[...] = run[tm - 1:tm, :]


def _forget_cumsum(xb, ss, w_f, b_f):
    s, d = xb.shape
    n = w_f.shape[1]
    tm = _tile(s, 512)
    return pl.pallas_call(
        functools.partial(_gate_kernel, tm=tm),
        grid=(s // tm,),
        in_specs=[
            pl.BlockSpec((tm, d), lambda i: (i, 0)),
            pl.BlockSpec((tm, LANE), lambda i: (i, 0)),
            pl.BlockSpec((d, n), lambda i: (0, 0)),
            pl.BlockSpec((1, n), lambda i: (0, 0)),
        ],
        out_specs=pl.BlockSpec((tm, n), lambda i: (i, 0)),
        out_shape=jax.ShapeDtypeStruct((s, n), F32),
        scratch_shapes=[pltpu.VMEM((1, n), F32)],
        compiler_params=_params("arbitrary"),
        name="forget_gate_cumsum",
    )(xb, ss, w_f, b_f)


GATE_PIECES = 3
V_ROWS = HEAD_DIM + 16
QUERY_TILE = 512


def _transposed_bf16(t):
    return t.astype(F32).T.astype(BF16)


def _attn_kernel(q_ref, qc_ref, k_ref, kc_ref, v_ref, o_ref, m_ref, acc_ref, qt_ref, vt_ref, *, tq, tk):
    i = pl.program_id(1)
    seq = v_ref.shape[0]

    @pl.when(i == 0)
    def _():
        def put(cidx, carry):
            at = pl.multiple_of(cidx * tk, tk)
            vt_ref[:HEAD_DIM, pl.ds(at, tk)] = _transposed_bf16(v_ref[pl.ds(at, tk), :])
            return carry

        lax.fori_loop(0, seq // tk, put, 0)
        extra = lax.broadcasted_iota(jnp.int32, (V_ROWS - HEAD_DIM, seq), 0)
        vt_ref[HEAD_DIM:, :] = jnp.where(extra == 0, 1.0, 0.0).astype(BF16)

    qt_ref[:HEAD_DIM, :] = _transposed_bf16(q_ref[...])
    qt_ref[HEAD_DIM:, :] = _transposed_bf16(qc_ref[0])
    m_ref[...] = jnp.full(m_ref.shape, -jnp.inf, F32)
    acc_ref[...] = jnp.zeros(acc_ref.shape, F32)

    def keys(start, rows):
        return jnp.concatenate([k_ref[pl.ds(start, rows), :], kc_ref[0, pl.ds(start, rows), :]], axis=1)

    def scores(n, k_rows):
        cols = slice(n * QUERY_TILE, (n + 1) * QUERY_TILE)
        return jnp.dot(k_rows, qt_ref[:, cols], preferred_element_type=F32)

    def attend(n, st, vt_cols, diag_off):
        cols = slice(n * QUERY_TILE, (n + 1) * QUERY_TILE)
        if diag_off is not None:
            key = lax.broadcasted_iota(jnp.int32, st.shape, 0)
            qry = lax.broadcasted_iota(jnp.int32, st.shape, 1) + diag_off
            st = jnp.where(key <= qry, st, -jnp.inf)
        m_old = m_ref[:, cols]
        m_new = jnp.maximum(m_old, jnp.max(st, axis=0, keepdims=True))
        pt = jnp.exp2(st - m_new).astype(BF16)
        alpha = jnp.exp2(m_old - m_new)
        acc_ref[:, cols] = alpha * acc_ref[:, cols] + jnp.dot(vt_cols, pt, preferred_element_type=F32)
        m_ref[:, cols] = m_new

    def run(units):
        loaded = {}

        def block_keys(start):
            if id(start) not in loaded:
                loaded[id(start)] = keys(start, tk)
            return loaded[id(start)]

        st = scores(units[0][0], block_keys(units[0][1]))
        st_next = None
        for u, (n, start, on_diag) in enumerate(units):
            if u + 1 < len(units):
                st_next = scores(units[u + 1][0], block_keys(units[u + 1][1]))
            attend(n, st, vt_ref[:, pl.ds(start, tk)], 0 if on_diag else None)
            st = st_next

    def body(jj, carry):
        starts = [pl.multiple_of((jj * blocks_per_trip + b) * tk, tk) for b in range(blocks_per_trip)]
        run([(n, start, False) for start in starts for n in range(nt)])
        return carry

    nt = tq // QUERY_TILE
    blocks_per_trip = tq // tk
    lax.fori_loop(0, i, body, 0)
    base = pl.multiple_of(i * tq, tq)
    diag_starts = [base + kb * tk for kb in range(nt)]
    run([(n, diag_starts[kb], kb == n) for kb in range(nt) for n in range(kb, nt)])
    inv_l = 1.0 / acc_ref[HEAD_DIM:HEAD_DIM + 1, :]
    o_ref[...] = (acc_ref[:HEAD_DIM, :] * inv_l).T.astype(BF16)


def _split_bf16(t):
    pieces = []
    for _ in range(GATE_PIECES):
        top = lax.bitcast_convert_type(
            lax.bitcast_convert_type(t, jnp.uint32) & jnp.uint32(0xFFFF0000), F32)
        pieces.append(top.astype(BF16))
        t = t - top
    return pieces


def _gate_columns(c, gate_first):
    cp = jnp.stack(_split_bf16(c * LOG2E), axis=-1)
    const = jnp.full(cp.shape, 1.0 if gate_first else -1.0, BF16)
    cols = jnp.concatenate([cp, const] if gate_first else [const, cp], axis=-1)
    cols = jnp.pad(cols, ((0, 0), (0, 0), (0, HEAD_DIM - 2 * GATE_PIECES)))
    return cols.transpose(1, 0, 2)


def _attention(q, kv, c):
    s, d = q.shape
    nh = d // HEAD_DIM
    tq = _tile(s, 2048)
    tk = QUERY_TILE
    assert tq % QUERY_TILE == 0
    return pl.pallas_call(
        functools.partial(_attn_kernel, tq=tq, tk=tk),
        grid=(nh, s // tq),
        in_specs=[
            pl.BlockSpec((tq, HEAD_DIM), lambda h, i: (i, h)),
            pl.BlockSpec((1, tq, HEAD_DIM), lambda h, i: (h, i, 0)),
            pl.BlockSpec((s, HEAD_DIM), lambda h, i: (0, h)),
            pl.BlockSpec((1, s, HEAD_DIM), lambda h, i: (h, 0, 0)),
            pl.BlockSpec((s, HEAD_DIM), lambda h, i: (0, nh + h)),
        ],
        out_specs=pl.BlockSpec((tq, HEAD_DIM), lambda h, i: (i, h)),
        out_shape=jax.ShapeDtypeStruct((s, d), BF16),
        scratch_shapes=[
            pltpu.VMEM((1, tq), F32),
            pltpu.VMEM((V_ROWS, tq), F32),
            pltpu.VMEM((2 * HEAD_DIM, tq), BF16),
            pltpu.VMEM((V_ROWS, s), BF16),
        ],
        compiler_params=_params("arbitrary", "arbitrary"),
        name="forgetting_attention",
    )(q, _gate_columns(c, False), kv, _gate_columns(c, True), kv)


def _pad_ff(t, f, fp, axis):
    pad = [(0, 0)] * t.ndim
    pad[axis] = (0, fp - f)
    return jnp.pad(t, pad)


def _split_pad(t, f, fp):
    return jnp.concatenate([_pad_ff(t[..., :f], f, fp, -1), _pad_ff(t[..., f:], f, fp, -1)], axis=-1)


def _gained(g, w):
    return (g[..., :, None] * w).astype(BF16)


def kernel(x, p, pool_norm, w_pool, pool_scale, attn_norm, w_q, q_norm, w_o, kv_norm, w_kvf, b_f,
           k_norm, ffn_norm, w_in, conv_w, conv_b, w_out, w_ple, gate_norm, w_gate):
    b, s, d = x.shape
    assert b == 1 and d % (HEAD_DIM * len(POOL_WINDOWS)) == 0
    depth = p.shape[0]
    n_a = pool_norm.shape[0]
    nh = d // HEAD_DIM
    assert nh <= LANE
    x = x[0]
    row = lambda t: t[None, :]
    f = w_out.shape[1]
    fp = -(-f // FF_ALIGN) * FF_ALIGN
    w_in_b = _split_pad(_gained(ffn_norm, w_in), f, fp)
    conv_w_p = _split_pad(conv_w, f, fp)
    conv_b_p = _split_pad(conv_b[:, None, :], f, fp)
    w_out_b = _pad_ff(w_out, f, fp, 1).astype(BF16)
    w_gate_b = _gained(gate_norm, w_gate)
    w_ple_b = w_ple.astype(BF16)
    w_o_b = w_o.astype(BF16)
    w_q_b = _gained(attn_norm, w_q)
    kv = c_heads = xb = ss = None
    for i in range(depth):
        if i < n_a:
            x, xb, ss = _pool_layer(x, row(pool_norm[i]), w_pool[i].astype(BF16), row(pool_scale[i]))
        else:
            j = i - n_a
            q = _proj_heads(xb, ss, w_q_b[j], row(q_norm[j]), d, d, HEAD_DIM ** -0.5 * LOG2E)
            att = _attention(q, kv, c_heads)
            x, xb, ss = _mm_res(att, w_o_b, j, x, 512, d)
        act = _ffn_in(xb, ss, w_in_b, conv_w_p, conv_b_p, i)
        x, xb, ss = _mm_res(act, w_out_b, i, x, 1024, fp // 4)
        x, xb, ss = _ple(xb, ss, w_gate_b, p, w_ple_b, i, x)
        if i == n_a - 1:
            w_kvf_g = _gained(kv_norm, w_kvf)
            kv = _proj_heads(xb, ss, w_kvf_g, row(k_norm), 2 * d, d, 1.0)
            w_f = jnp.pad(w_kvf_g[:, 2 * d:], ((0, 0), (0, LANE - nh)))
            c = _forget_cumsum(xb, ss, w_f, jnp.pad(b_f, (0, LANE - nh))[None, :])
            c_heads = c[:, :nh]
    return x[None]
```

```python
import functools
import math

import jax
import jax.numpy as jnp
from jax import lax
from jax.experimental import pallas as pl
from jax.experimental.pallas import tpu as pltpu

HEAD_DIM = 128
POOL_WINDOWS = (2, 4, 8, 16)
POOL_HALO = 16
CONV_WIDTH = 3
CARRY_ROWS = 8
EPS = 1e-6
LOG2E = math.log2(math.e)
LANE = 128
VMEM_LIMIT = 56 * 1024 * 1024

F32 = jnp.float32
BF16 = jnp.bfloat16


def _tile(n, want):
    t = min(n, want)
    while n % t:
        t //= 2
    return t


def _params(*sem):
    return pltpu.CompilerParams(dimension_semantics=sem, vmem_limit_bytes=VMEM_LIMIT)


def _rms(x, g):
    ms = jnp.mean(x * x, axis=-1, keepdims=True)
    return x * lax.rsqrt(ms + EPS) * g


def _rstd(ss_ref, d):
    return lax.rsqrt(ss_ref[:, :1] * (1.0 / d) + EPS)


def _emit_stream(out, j, o_ref, ob_ref, ss_ref):
    o_ref[...] = out
    ob_ref[...] = out.astype(BF16)
    part = jnp.broadcast_to(jnp.sum(out * out, axis=1, keepdims=True), ss_ref.shape)

    @pl.when(j == 0)
    def _():
        ss_ref[...] = part

    @pl.when(j > 0)
    def _():
        ss_ref[...] += part


def _stream_shapes(s, d):
    return [jax.ShapeDtypeStruct((s, d), F32), jax.ShapeDtypeStruct((s, d), BF16),
            jax.ShapeDtypeStruct((s, LANE), F32)]


def _pool_kernel(x_ref, halo_ref, g_ref, w_ref, sc_ref, xo_ref, xb_ref, ss_ref, *, tm):
    i = pl.program_id(0)
    x = x_ref[...]
    g = g_ref[...]
    h = _rms(x, g)
    hh = jnp.where(i > 0, _rms(halo_ref[...], g), 0.0)
    cur = jnp.concatenate([hh, h], axis=0)
    d = x.shape[1]
    c = d // len(POOL_WINDOWS)
    pos = i * tm + lax.broadcasted_iota(jnp.int32, (tm, 1), 0)
    outs = []
    for gi, w in enumerate(POOL_WINDOWS):
        cur = cur + pltpu.roll(cur, w // 2, axis=0)
        win = cur[POOL_HALO:, :c]
        if gi + 1 < len(POOL_WINDOWS):
            cur = cur[:, c:]
        inv_cnt = 1.0 / jnp.minimum(pos + 1, w).astype(F32)
        pooled = win * inv_cnt - h[:, gi * c:(gi + 1) * c]
        outs.append(jnp.dot(pooled.astype(BF16), w_ref[gi], preferred_element_type=F32))
    x1 = x + jnp.concatenate(outs, axis=1) * sc_ref[...]
    xo_ref[...] = x1
    xb_ref[...] = x1.astype(BF16)
    ss_ref[...] = jnp.broadcast_to(jnp.sum(x1 * x1, axis=1, keepdims=True), ss_ref.shape)


def _pool_layer(x, g, w_pool, scale):
    s, d = x.shape
    tm = _tile(s, 256)
    hb = tm // POOL_HALO
    row = lambda i: (i, 0)
    const2 = lambda i: (0, 0)
    return pl.pallas_call(
        functools.partial(_pool_kernel, tm=tm),
        grid=(s // tm,),
        in_specs=[
            pl.BlockSpec((tm, d), row),
            pl.BlockSpec((POOL_HALO, d), lambda i: (jnp.maximum(i * hb - 1, 0), 0)),
            pl.BlockSpec((1, d), const2),
            pl.BlockSpec(w_pool.shape, lambda i: (0, 0, 0)),
            pl.BlockSpec((1, d), const2),
        ],
        out_specs=[pl.BlockSpec((tm, d), row), pl.BlockSpec((tm, d), row), pl.BlockSpec((tm, LANE), row)],
        out_shape=_stream_shapes(s, d),
        compiler_params=_params("arbitrary"),
        name="pool_mixer",
    )(x, x, g, w_pool, scale)


def _causal_conv(u, prev, cw, b):
    tm = u.shape[0]
    ext = jnp.concatenate([prev, u], axis=0)
    u1 = pltpu.roll(ext, 1, axis=0)[CARRY_ROWS:]
    u2 = pltpu.roll(ext, 2, axis=0)[CARRY_ROWS:]
    return cw[2:3] * u + cw[1:2] * u1 + cw[0:1] * u2 + b, ext[tm:]


def _ffn_in_kernel(a_ref, ss_ref, wg_ref, wv_ref, cwg_ref, cwv_ref, bg_ref, bv_ref, o_ref, carry_ref):
    i = pl.program_id(0)
    j = pl.program_id(1)

    @pl.when(i == 0)
    def _():
        carry_ref[j] = jnp.zeros(carry_ref.shape[1:], F32)

    a = a_ref[...]
    rstd = _rstd(ss_ref, a.shape[1])
    ug = jnp.dot(a, wg_ref[...], preferred_element_type=F32) * rstd
    uv = jnp.dot(a, wv_ref[...], preferred_element_type=F32) * rstd
    yg, tail_g = _causal_conv(ug, carry_ref[j, 0], cwg_ref[...], bg_ref[...])
    yv, tail_v = _causal_conv(uv, carry_ref[j, 1], cwv_ref[...], bv_ref[...])
    carry_ref[j, 0] = tail_g
    carry_ref[j, 1] = tail_v
    o_ref[...] = (yg * jax.nn.sigmoid(yg) * yv).astype(BF16)


def _ffn_in(xb, ss, w_gate, w_val, cw_gate, cw_val, b_gate, b_val, layer):
    s, d = xb.shape
    f = w_gate.shape[2]
    tm = _tile(s, 1024)
    tn = min(f, 512)
    assert tn % LANE == 0
    nj = pl.cdiv(f, tn)
    col = lambda i, j: (layer, 0, j)
    return pl.pallas_call(
        _ffn_in_kernel,
        grid=(s // tm, nj),
        in_specs=[
            pl.BlockSpec((tm, d), lambda i, j: (i, 0)),
            pl.BlockSpec((tm, LANE), lambda i, j: (i, 0)),
            pl.BlockSpec((None, d, tn), col),
            pl.BlockSpec((None, d, tn), col),
            pl.BlockSpec((None, CONV_WIDTH, tn), col),
            pl.BlockSpec((None, CONV_WIDTH, tn), col),
            pl.BlockSpec((None, 1, tn), col),
            pl.BlockSpec((None, 1, tn), col),
        ],
        out_specs=pl.BlockSpec((tm, tn), lambda i, j: (i, j)),
        out_shape=jax.ShapeDtypeStruct((s, f), BF16),
        scratch_shapes=[pltpu.VMEM((nj, 2, CARRY_ROWS, tn), F32)],
        compiler_params=_params("arbitrary", "arbitrary"),
        name="ffn_in_conv_act",
    )(xb, ss, w_gate, w_val, cw_gate, cw_val, b_gate, b_val)


def _mm_res_kernel(a_ref, w_ref, r_ref, o_ref, ob_ref, ss_ref, acc_ref, *, nk):
    j = pl.program_id(1)
    k = pl.program_id(2)
    prod = jnp.dot(a_ref[...], w_ref[...], preferred_element_type=F32)
    if nk == 1:
        _emit_stream(r_ref[...] + prod, j, o_ref, ob_ref, ss_ref)
        return

    @pl.when(k == 0)
    def _():
        acc_ref[...] = prod

    @pl.when(jnp.logical_and(k > 0, k < nk - 1))
    def _():
        acc_ref[...] += prod

    @pl.when(k == nk - 1)
    def _():
        _emit_stream(r_ref[...] + (acc_ref[...] + prod), j, o_ref, ob_ref, ss_ref)


def _mm_res(a, w, layer, res, tn_want, tk_want):
    s, kdim = a.shape
    n = w.shape[2]
    tm = _tile(s, 1024)
    tn = _tile(n, tn_want)
    tk = _tile(kdim, tk_want)
    assert tk % LANE == 0
    nk = kdim // tk
    tile = pl.BlockSpec((tm, tn), lambda i, j, k: (i, j))
    return pl.pallas_call(
        functools.partial(_mm_res_kernel, nk=nk),
        grid=(s // tm, n // tn, nk),
        in_specs=[
            pl.BlockSpec((tm, tk), lambda i, j, k: (i, k)),
            pl.BlockSpec((None, tk, tn), lambda i, j, k: (layer, k, j)),
            tile,
        ],
        out_specs=[tile, tile, pl.BlockSpec((tm, LANE), lambda i, j, k: (i, 0))],
        out_shape=_stream_shapes(s, n),
        scratch_shapes=[pltpu.VMEM((tm, tn), F32)],
        compiler_params=_params("arbitrary", "arbitrary", "arbitrary"),
        name="matmul_residual",
    )(a, w, res)


def _ple_kernel(a_ref, ss_ref, wg_ref, p_ref, wp_ref, x_ref, o_ref, ob_ref, sso_ref):
    a = a_ref[...]
    z = jnp.dot(a, wg_ref[...], preferred_element_type=F32) * _rstd(ss_ref, a.shape[1])
    e = jnp.dot(p_ref[...].astype(BF16), wp_ref[...], preferred_element_type=F32)
    _emit_stream(x_ref[...] + jax.nn.sigmoid(z) * e, pl.program_id(1), o_ref, ob_ref, sso_ref)


def _ple(xb, ss, w_gate, p, w_ple, layer, x):
    s, d = x.shape
    pd = p.shape[3]
    tm = _tile(s, 1024)
    tn = _tile(d, 512)
    tile = pl.BlockSpec((tm, tn), lambda i, j: (i, j))
    stats = pl.BlockSpec((tm, LANE), lambda i, j: (i, 0))
    return pl.pallas_call(
        _ple_kernel,
        grid=(s // tm, d // tn),
        in_specs=[
            pl.BlockSpec((tm, d), lambda i, j: (i, 0)),
            stats,
            pl.BlockSpec((None, d, tn), lambda i, j: (layer, 0, j)),
            pl.BlockSpec((None, None, tm, pd), lambda i, j: (layer, 0, i, 0)),
            pl.BlockSpec((None, pd, tn), lambda i, j: (layer, 0, j)),
            tile,
        ],
        out_specs=[tile, tile, stats],
        out_shape=_stream_shapes(s, d),
        compiler_params=_params("arbitrary", "arbitrary"),
        name="per_layer_embed",
    )(xb, ss, w_gate, p, w_ple, x)


def _proj_kernel(a_ref, ss_ref, w_ref, g_ref, o_ref, *, n_norm_tiles, out_scale):
    j = pl.program_id(1)
    a = a_ref[...]
    z = jnp.dot(a, w_ref[...], preferred_element_type=F32) * _rstd(ss_ref, a.shape[1])
    tn = z.shape[1]

    @pl.when(j < n_norm_tiles)
    def _():
        g = g_ref[...] * out_scale
        for hh in range(tn // HEAD_DIM):
            sl = slice(hh * HEAD_DIM, (hh + 1) * HEAD_DIM)
            zh = z[:, sl]
            ms = jnp.mean(zh * zh, axis=-1, keepdims=True)
            o_ref[:, sl] = (zh * lax.rsqrt(ms + EPS) * g).astype(BF16)

    @pl.when(j >= n_norm_tiles)
    def _():
        o_ref[...] = z.astype(BF16)


def _proj_heads(xb, ss, w, head_gain, n, n_norm_cols, out_scale):
    s, d = xb.shape
    tm = _tile(s, 1024)
    tn = _tile(math.gcd(n, n_norm_cols), 1024)
    assert tn % HEAD_DIM == 0
    return pl.pallas_call(
        functools.partial(_proj_kernel, n_norm_tiles=n_norm_cols // tn, out_scale=out_scale),
        grid=(s // tm, n // tn),
        in_specs=[
            pl.BlockSpec((tm, d), lambda i, j: (i, 0)),
            pl.BlockSpec((tm, LANE), lambda i, j: (i, 0)),
            pl.BlockSpec((d, tn), lambda i, j: (0, j)),
            pl.BlockSpec((1, HEAD_DIM), lambda i, j: (0, 0)),
        ],
        out_specs=pl.BlockSpec((tm, tn), lambda i, j: (i, j)),
        out_shape=jax.ShapeDtypeStruct((s, n), BF16),
        compiler_params=_params("arbitrary", "arbitrary"),
        name="head_projection",
    )(xb, ss, w, head_gain)


def _gate_kernel(a_ref, ss_ref, w_ref, b_ref, o_ref, carry_ref, *, tm):
    i = pl.program_id(0)

    @pl.when(i == 0)
    def _():
        carry_ref[...] = jnp.zeros(carry_ref.shape, F32)

    a = a_ref[...]
    z = jnp.dot(a, w_ref[...], preferred_element_type=F32) * _rstd(ss_ref, a.shape[1]) + b_ref[...]
    run = jnp.minimum(z, 0.0) - jnp.log1p(jnp.exp(-jnp.abs(z)))
    row = lax.broadcasted_iota(jnp.int32, run.shape, 0)
    shift = 1
    while shift < tm:
        run = run + jnp.where(row >= shift, pltpu.roll(run, shift, axis=0), 0.0)
        shift *= 2
    run = run + carry_ref[...]
    o_ref[...] = run
    carry_ref[...] = run[tm - 1:tm, :]


def _forget_cumsum(xb, ss, w_f, b_f):
    s, d = xb.shape
    n = w_f.shape[1]
    tm = _tile(s, 512)
    return pl.pallas_call(
        functools.partial(_gate_kernel, tm=tm),
        grid=(s // tm,),
        in_specs=[
            pl.BlockSpec((tm, d), lambda i: (i, 0)),
            pl.BlockSpec((tm, LANE), lambda i: (i, 0)),
            pl.BlockSpec((d, n), lambda i: (0, 0)),
            pl.BlockSpec((1, n), lambda i: (0, 0)),
        ],
        out_specs=pl.BlockSpec((tm, n), lambda i: (i, 0)),
        out_shape=jax.ShapeDtypeStruct((s, n), F32),
        scratch_shapes=[pltpu.VMEM((1, n), F32)],
        compiler_params=_params("arbitrary"),
        name="forget_gate_cumsum",
    )(xb, ss, w_f, b_f)


GATE_PIECES = 3
V_ROWS = HEAD_DIM + 16
QUERY_TILE = 512


def _transposed_bf16(t):
    return t.astype(F32).T.astype(BF16)


def _attn_kernel(q_ref, qc_ref, k_ref, kc_ref, v_ref, o_ref, m_ref, acc_ref, qt_ref, vt_ref, *, tq, tk):
    i = pl.program_id(1)
    seq = v_ref.shape[0]

    @pl.when(i == 0)
    def _():
        def put(cidx, carry):
            at = pl.multiple_of(cidx * tk, tk)
            vt_ref[:HEAD_DIM, pl.ds(at, tk)] = _transposed_bf16(v_ref[pl.ds(at, tk), :])
            return carry

        lax.fori_loop(0, seq // tk, put, 0)
        extra = lax.broadcasted_iota(jnp.int32, (V_ROWS - HEAD_DIM, seq), 0)
        vt_ref[HEAD_DIM:, :] = jnp.where(extra == 0, 1.0, 0.0).astype(BF16)

    qt_ref[:HEAD_DIM, :] = _transposed_bf16(q_ref[...])
    qt_ref[HEAD_DIM:, :] = _transposed_bf16(qc_ref[0])
    m_ref[...] = jnp.full(m_ref.shape, -jnp.inf, F32)
    acc_ref[...] = jnp.zeros(acc_ref.shape, F32)

    def keys(start, rows):
        return jnp.concatenate([k_ref[pl.ds(start, rows), :], kc_ref[0, pl.ds(start, rows), :]], axis=1)

    def scores(n, k_rows):
        cols = slice(n * QUERY_TILE, (n + 1) * QUERY_TILE)
        return jnp.dot(k_rows, qt_ref[:, cols], preferred_element_type=F32)

    def attend(n, st, vt_cols, diag_off):
        cols = slice(n * QUERY_TILE, (n + 1) * QUERY_TILE)
        if diag_off is not None:
            key = lax.broadcasted_iota(jnp.int32, st.shape, 0)
            qry = lax.broadcasted_iota(jnp.int32, st.shape, 1) + diag_off
            st = jnp.where(key <= qry, st, -jnp.inf)
        m_old = m_ref[:, cols]
        m_new = jnp.maximum(m_old, jnp.max(st, axis=0, keepdims=True))
        pt = jnp.exp2(st - m_new).astype(BF16)
        alpha = jnp.exp2(m_old - m_new)
        acc_ref[:, cols] = alpha * acc_ref[:, cols] + jnp.dot(vt_cols, pt, preferred_element_type=F32)
        m_ref[:, cols] = m_new

    def run(units):
        loaded = {}

        def block_keys(start):
            if id(start) not in loaded:
                loaded[id(start)] = keys(start, tk)
            return loaded[id(start)]

        st = scores(units[0][0], block_keys(units[0][1]))
        st_next = None
        for u, (n, start, on_diag) in enumerate(units):
            if u + 1 < len(units):
                st_next = scores(units[u + 1][0], block_keys(units[u + 1][1]))
            attend(n, st, vt_ref[:, pl.ds(start, tk)], 0 if on_diag else None)
            st = st_next

    def body(jj, carry):
        starts = [pl.multiple_of((jj * blocks_per_trip + b) * tk, tk) for b in range(blocks_per_trip)]
        run([(n, start, False) for start in starts for n in range(nt)])
        return carry

    nt = tq // QUERY_TILE
    blocks_per_trip = tq // tk
    lax.fori_loop(0, i, body, 0)
    base = pl.multiple_of(i * tq, tq)
    diag_starts = [base + kb * tk for kb in range(nt)]
    run([(n, diag_starts[kb], kb == n) for kb in range(nt) for n in range(kb, nt)])
    inv_l = 1.0 / acc_ref[HEAD_DIM:HEAD_DIM + 1, :]
    o_ref[...] = (acc_ref[:HEAD_DIM, :] * inv_l).T.astype(BF16)


def _split_bf16(t):
    pieces = []
    for _ in range(GATE_PIECES):
        top = lax.bitcast_convert_type(
            lax.bitcast_convert_type(t, jnp.uint32) & jnp.uint32(0xFFFF0000), F32)
        pieces.append(top.astype(BF16))
        t = t - top
    return pieces


def _gate_columns(c, gate_first):
    cp = jnp.stack(_split_bf16(c * LOG2E), axis=-1)
    const = jnp.full(cp.shape, 1.0 if gate_first else -1.0, BF16)
    cols = jnp.concatenate([cp, const] if gate_first else [const, cp], axis=-1)
    cols = jnp.pad(cols, ((0, 0), (0, 0), (0, HEAD_DIM - 2 * GATE_PIECES)))
    return cols.transpose(1, 0, 2)


def _attention(q, kv, c):
    s, d = q.shape
    nh = d // HEAD_DIM
    tq = _tile(s, 2048)
    tk = QUERY_TILE
    assert tq % QUERY_TILE == 0
    return pl.pallas_call(
        functools.partial(_attn_kernel, tq=tq, tk=tk),
        grid=(nh, s // tq),
        in_specs=[
            pl.BlockSpec((tq, HEAD_DIM), lambda h, i: (i, h)),
            pl.BlockSpec((1, tq, HEAD_DIM), lambda h, i: (h, i, 0)),
            pl.BlockSpec((s, HEAD_DIM), lambda h, i: (0, h)),
            pl.BlockSpec((1, s, HEAD_DIM), lambda h, i: (h, 0, 0)),
            pl.BlockSpec((s, HEAD_DIM), lambda h, i: (0, nh + h)),
        ],
        out_specs=pl.BlockSpec((tq, HEAD_DIM), lambda h, i: (i, h)),
        out_shape=jax.ShapeDtypeStruct((s, d), BF16),
        scratch_shapes=[
            pltpu.VMEM((1, tq), F32),
            pltpu.VMEM((V_ROWS, tq), F32),
            pltpu.VMEM((2 * HEAD_DIM, tq), BF16),
            pltpu.VMEM((V_ROWS, s), BF16),
        ],
        compiler_params=_params("arbitrary", "arbitrary"),
        name="forgetting_attention",
    )(q, _gate_columns(c, False), kv, _gate_columns(c, True), kv)


def _gained(g, w):
    return (g[..., :, None] * w).astype(BF16)


def kernel(x, p, pool_norm, w_pool, pool_scale, attn_norm, w_q, q_norm, w_o, kv_norm, w_kvf, b_f,
           k_norm, ffn_norm, w_in, conv_w, conv_b, w_out, w_ple, gate_norm, w_gate):
    b, s, d = x.shape
    assert b == 1 and d % (HEAD_DIM * len(POOL_WINDOWS)) == 0
    depth = p.shape[0]
    n_a = pool_norm.shape[0]
    nh = d // HEAD_DIM
    assert nh <= LANE
    x = x[0]
    row = lambda t: t[None, :]
    f = w_out.shape[1]
    halves = lambda t: (t[..., :f], t[..., f:])
    w_in_halves = [_gained(ffn_norm, t) for t in halves(w_in)]
    ffn_in_consts = (*w_in_halves, *halves(conv_w), *halves(conv_b[:, None, :]))
    w_out_b = w_out.astype(BF16)
    w_gate_b = _gained(gate_norm, w_gate)
    w_ple_b = w_ple.astype(BF16)
    w_o_b = w_o.astype(BF16)
    w_q_b = _gained(attn_norm, w_q)
    kv = c_heads = xb = ss = None
    for i in range(depth):
        if i < n_a:
            x, xb, ss = _pool_layer(x, row(pool_norm[i]), w_pool[i].astype(BF16), row(pool_scale[i]))
        else:
            j = i - n_a
            q = _proj_heads(xb, ss, w_q_b[j], row(q_norm[j]), d, d, HEAD_DIM ** -0.5 * LOG2E)
            att = _attention(q, kv, c_heads)
            x, xb, ss = _mm_res(att, w_o_b, j, x, 512, d)
        act = _ffn_in(xb, ss, *ffn_in_consts, i)
        x, xb, ss = _mm_res(act, w_out_b, i, x, 512, f // 2 if f % (2 * LANE) == 0 else f)
        x, xb, ss = _ple(xb, ss, w_gate_b, p, w_ple_b, i, x)
        if i == n_a - 1:
            w_kvf_g = _gained(kv_norm, w_kvf)
            kv = _proj_heads(xb, ss, w_kvf_g, row(k_norm), 2 * d, d, 1.0)
            w_f = jnp.pad(w_kvf_g[:, 2 * d:], ((0, 0), (0, LANE - nh)))
            c = _forget_cumsum(xb, ss, w_f, jnp.pad(b_f, (0, LANE - nh))[None, :])
            c_heads = c[:, :nh]
    return x[None]
```

```python
import functools
import math

import jax
import jax.numpy as jnp
from jax import lax
from jax.experimental import pallas as pl
from jax.experimental.pallas import tpu as pltpu

HEAD_DIM = 128
POOL_WINDOWS = (2, 4, 8, 16)
POOL_HALO = 16
CONV_WIDTH = 3
CARRY_ROWS = 8
EPS = 1e-6
LOG2E = math.log2(math.e)
LANE = 128
VMEM_LIMIT = 56 * 1024 * 1024

F32 = jnp.float32
BF16 = jnp.bfloat16


def _tile(n, want):
    t = min(n, want)
    while n % t:
        t //= 2
    return t


def _params(*sem):
    return pltpu.CompilerParams(dimension_semantics=sem, vmem_limit_bytes=VMEM_LIMIT)


def _rms(x, g):
    ms = jnp.mean(x * x, axis=-1, keepdims=True)
    return x * lax.rsqrt(ms + EPS) * g


def _rstd(ss_ref, d):
    return lax.rsqrt(ss_ref[:, :1] * (1.0 / d) + EPS)


def _emit_stream(out, j, o_ref, ob_ref, ss_ref):
    o_ref[...] = out
    ob_ref[...] = out.astype(BF16)
    part = jnp.broadcast_to(jnp.sum(out * out, axis=1, keepdims=True), ss_ref.shape)

    @pl.when(j == 0)
    def _():
        ss_ref[...] = part

    @pl.when(j > 0)
    def _():
        ss_ref[...] += part


def _stream_shapes(s, d):
    return [jax.ShapeDtypeStruct((s, d), F32), jax.ShapeDtypeStruct((s, d), BF16),
            jax.ShapeDtypeStruct((s, LANE), F32)]


def _pool_kernel(x_ref, halo_ref, g_ref, w_ref, sc_ref, xo_ref, xb_ref, ss_ref, *, tm):
    i = pl.program_id(0)
    x = x_ref[...]
    g = g_ref[...]
    h = _rms(x, g)
    hh = jnp.where(i > 0, _rms(halo_ref[...], g), 0.0)
    cur = jnp.concatenate([hh, h], axis=0)
    d = x.shape[1]
    c = d // len(POOL_WINDOWS)
    pos = i * tm + lax.broadcasted_iota(jnp.int32, (tm, 1), 0)
    outs = []
    for gi, w in enumerate(POOL_WINDOWS):
        cur = cur + pltpu.roll(cur, w // 2, axis=0)
        win = cur[POOL_HALO:, :c]
        if gi + 1 < len(POOL_WINDOWS):
            cur = cur[:, c:]
        inv_cnt = 1.0 / jnp.minimum(pos + 1, w).astype(F32)
        pooled = win * inv_cnt - h[:, gi * c:(gi + 1) * c]
        outs.append(jnp.dot(pooled.astype(BF16), w_ref[gi], preferred_element_type=F32))
    x1 = x + jnp.concatenate(outs, axis=1) * sc_ref[...]
    xo_ref[...] = x1
    xb_ref[...] = x1.astype(BF16)
    ss_ref[...] = jnp.broadcast_to(jnp.sum(x1 * x1, axis=1, keepdims=True), ss_ref.shape)


def _pool_layer(x, g, w_pool, scale):
    s, d = x.shape
    tm = _tile(s, 256)
    hb = tm // POOL_HALO
    row = lambda i: (i, 0)
    const2 = lambda i: (0, 0)
    return pl.pallas_call(
        functools.partial(_pool_kernel, tm=tm),
        grid=(s // tm,),
        in_specs=[
            pl.BlockSpec((tm, d), row),
            pl.BlockSpec((POOL_HALO, d), lambda i: (jnp.maximum(i * hb - 1, 0), 0)),
            pl.BlockSpec((1, d), const2),
            pl.BlockSpec(w_pool.shape, lambda i: (0, 0, 0)),
            pl.BlockSpec((1, d), const2),
        ],
        out_specs=[pl.BlockSpec((tm, d), row), pl.BlockSpec((tm, d), row), pl.BlockSpec((tm, LANE), row)],
        out_shape=_stream_shapes(s, d),
        compiler_params=_params("arbitrary"),
        name="pool_mixer",
    )(x, x, g, w_pool, scale)


def _causal_conv(u, prev, cw, b):
    tm = u.shape[0]
    ext = jnp.concatenate([prev, u], axis=0)
    u1 = pltpu.roll(ext, 1, axis=0)[CARRY_ROWS:]
    u2 = pltpu.roll(ext, 2, axis=0)[CARRY_ROWS:]
    return cw[2:3] * u + cw[1:2] * u1 + cw[0:1] * u2 + b, ext[tm:]


def _ffn_in_kernel(a_ref, ss_ref, wg_ref, wv_ref, cwg_ref, cwv_ref, bg_ref, bv_ref, o_ref, carry_ref):
    i = pl.program_id(0)
    j = pl.program_id(1)

    @pl.when(i == 0)
    def _():
        carry_ref[j] = jnp.zeros(carry_ref.shape[1:], F32)

    a = a_ref[...]
    rstd = _rstd(ss_ref, a.shape[1])
    ug = jnp.dot(a, wg_ref[...], preferred_element_type=F32) * rstd
    uv = jnp.dot(a, wv_ref[...], preferred_element_type=F32) * rstd
    yg, tail_g = _causal_conv(ug, carry_ref[j, 0], cwg_ref[...], bg_ref[...])
    yv, tail_v = _causal_conv(uv, carry_ref[j, 1], cwv_ref[...], bv_ref[...])
    carry_ref[j, 0] = tail_g
    carry_ref[j, 1] = tail_v
    o_ref[...] = (yg * jax.nn.sigmoid(yg) * yv).astype(BF16)


def _ffn_in(xb, ss, w_gate, w_val, cw_gate, cw_val, b_gate, b_val, layer):
    s, d = xb.shape
    f = w_gate.shape[2]
    tm = _tile(s, 1024)
    tn = min(f, 512)
    assert tn % LANE == 0
    nj = pl.cdiv(f, tn)
    col = lambda i, j: (layer, 0, j)
    return pl.pallas_call(
        _ffn_in_kernel,
        grid=(s // tm, nj),
        in_specs=[
            pl.BlockSpec((tm, d), lambda i, j: (i, 0)),
            pl.BlockSpec((tm, LANE), lambda i, j: (i, 0)),
            pl.BlockSpec((None, d, tn), col),
            pl.BlockSpec((None, d, tn), col),
            pl.BlockSpec((None, CONV_WIDTH, tn), col),
            pl.BlockSpec((None, CONV_WIDTH, tn), col),
            pl.BlockSpec((None, 1, tn), col),
            pl.BlockSpec((None, 1, tn), col),
        ],
        out_specs=pl.BlockSpec((tm, tn), lambda i, j: (i, j)),
        out_shape=jax.ShapeDtypeStruct((s, f), BF16),
        scratch_shapes=[pltpu.VMEM((nj, 2, CARRY_ROWS, tn), F32)],
        compiler_params=_params("arbitrary", "arbitrary"),
        name="ffn_in_conv_act",
    )(xb, ss, w_gate, w_val, cw_gate, cw_val, b_gate, b_val)


def _mm_res_kernel(a_ref, w_ref, r_ref, o_ref, ob_ref, ss_ref, *acc, nk, tn):
    k = pl.program_id(1)
    j = pl.program_id(2)
    prod = jnp.dot(a_ref[...], w_ref[...], preferred_element_type=F32)
    if nk == 1:
        _emit_stream(r_ref[...] + prod, j, o_ref, ob_ref, ss_ref)
        return
    (acc_ref,) = acc
    cols = pl.ds(pl.multiple_of(j * tn, tn), tn)

    @pl.when(k == 0)
    def _():
        acc_ref[:, cols] = prod

    @pl.when(jnp.logical_and(k > 0, k < nk - 1))
    def _():
        acc_ref[:, cols] += prod

    @pl.when(k == nk - 1)
    def _():
        _emit_stream(r_ref[...] + (acc_ref[:, cols] + prod), j, o_ref, ob_ref, ss_ref)


def _mm_res(a, w, layer, res, tm_want, tn_want, tk_want):
    s, kdim = a.shape
    n = w.shape[2]
    tm = _tile(s, tm_want)
    tn = _tile(n, tn_want)
    tk = _tile(kdim, tk_want)
    assert tk % LANE == 0
    nk = kdim // tk
    tile = pl.BlockSpec((tm, tn), lambda i, k, j: (i, jnp.where(k == nk - 1, j, 0)))
    return pl.pallas_call(
        functools.partial(_mm_res_kernel, nk=nk, tn=tn),
        grid=(s // tm, nk, n // tn),
        in_specs=[
            pl.BlockSpec((tm, tk), lambda i, k, j: (i, k)),
            pl.BlockSpec((None, tk, tn), lambda i, k, j: (layer, k, j)),
            tile,
        ],
        out_specs=[tile, tile, pl.BlockSpec((tm, LANE), lambda i, k, j: (i, 0))],
        out_shape=_stream_shapes(s, n),
        scratch_shapes=[pltpu.VMEM((tm, n), F32)] if nk > 1 else [],
        compiler_params=_params("arbitrary", "arbitrary", "arbitrary"),
        name="matmul_residual",
    )(a, w, res)


def _ple_kernel(a_ref, ss_ref, wg_ref, p_ref, wp_ref, x_ref, o_ref, ob_ref, sso_ref):
    a = a_ref[...]
    z = jnp.dot(a, wg_ref[...], preferred_element_type=F32) * _rstd(ss_ref, a.shape[1])
    e = jnp.dot(p_ref[...].astype(BF16), wp_ref[...], preferred_element_type=F32)
    _emit_stream(x_ref[...] + jax.nn.sigmoid(z) * e, pl.program_id(1), o_ref, ob_ref, sso_ref)


def _ple(xb, ss, w_gate, p, w_ple, layer, x):
    s, d = x.shape
    pd = p.shape[3]
    tm = _tile(s, 1024)
    tn = _tile(d, 512)
    tile = pl.BlockSpec((tm, tn), lambda i, j: (i, j))
    stats = pl.BlockSpec((tm, LANE), lambda i, j: (i, 0))
    return pl.pallas_call(
        _ple_kernel,
        grid=(s // tm, d // tn),
        in_specs=[
            pl.BlockSpec((tm, d), lambda i, j: (i, 0)),
            stats,
            pl.BlockSpec((None, d, tn), lambda i, j: (layer, 0, j)),
            pl.BlockSpec((None, None, tm, pd), lambda i, j: (layer, 0, i, 0)),
            pl.BlockSpec((None, pd, tn), lambda i, j: (layer, 0, j)),
            tile,
        ],
        out_specs=[tile, tile, stats],
        out_shape=_stream_shapes(s, d),
        compiler_params=_params("arbitrary", "arbitrary"),
        name="per_layer_embed",
    )(xb, ss, w_gate, p, w_ple, x)


def _proj_kernel(a_ref, ss_ref, w_ref, g_ref, o_ref, *, n_norm_tiles, out_scale):
    j = pl.program_id(1)
    a = a_ref[...]
    z = jnp.dot(a, w_ref[...], preferred_element_type=F32) * _rstd(ss_ref, a.shape[1])
    tn = z.shape[1]

    @pl.when(j < n_norm_tiles)
    def _():
        g = g_ref[...] * out_scale
        for hh in range(tn // HEAD_DIM):
            sl = slice(hh * HEAD_DIM, (hh + 1) * HEAD_DIM)
            zh = z[:, sl]
            ms = jnp.mean(zh * zh, axis=-1, keepdims=True)
            o_ref[:, sl] = (zh * lax.rsqrt(ms + EPS) * g).astype(BF16)

    @pl.when(j >= n_norm_tiles)
    def _():
        o_ref[...] = z.astype(BF16)


def _proj_heads(xb, ss, w, head_gain, n, n_norm_cols, out_scale):
    s, d = xb.shape
    tm = _tile(s, 1024)
    tn = _tile(math.gcd(n, n_norm_cols), 1024)
    assert tn % HEAD_DIM == 0
    return pl.pallas_call(
        functools.partial(_proj_kernel, n_norm_tiles=n_norm_cols // tn, out_scale=out_scale),
        grid=(s // tm, n // tn),
        in_specs=[
            pl.BlockSpec((tm, d), lambda i, j: (i, 0)),
            pl.BlockSpec((tm, LANE), lambda i, j: (i, 0)),
            pl.BlockSpec((d, tn), lambda i, j: (0, j)),
            pl.BlockSpec((1, HEAD_DIM), lambda i, j: (0, 0)),
        ],
        out_specs=pl.BlockSpec((tm, tn), lambda i, j: (i, j)),
        out_shape=jax.ShapeDtypeStruct((s, n), BF16),
        compiler_params=_params("arbitrary", "arbitrary"),
        name="head_projection",
    )(xb, ss, w, head_gain)


def _gate_kernel(a_ref, ss_ref, w_ref, b_ref, o_ref, carry_ref, *, tm):
    i = pl.program_id(0)

    @pl.when(i == 0)
    def _():
        carry_ref[...] = jnp.zeros(carry_ref.shape, F32)

    a = a_ref[...]
    z = jnp.dot(a, w_ref[...], preferred_element_type=F32) * _rstd(ss_ref, a.shape[1]) + b_ref[...]
    run = jnp.minimum(z, 0.0) - jnp.log1p(jnp.exp(-jnp.abs(z)))
    row = lax.broadcasted_iota(jnp.int32, run.shape, 0)
    shift = 1
    while shift < tm:
        run = run + jnp.where(row >= shift, pltpu.roll(run, shift, axis=0), 0.0)
        shift *= 2
    run = run + carry_ref[...]
    o_ref[...] = run
    carry_ref[...] = run[tm - 1:tm, :]


def _forget_cumsum(xb, ss, w_f, b_f):
    s, d = xb.shape
    n = w_f.shape[1]
    tm = _tile(s, 512)
    return pl.pallas_call(
        functools.partial(_gate_kernel, tm=tm),
        grid=(s // tm,),
        in_specs=[
            pl.BlockSpec((tm, d), lambda i: (i, 0)),
            pl.BlockSpec((tm, LANE), lambda i: (i, 0)),
            pl.BlockSpec((d, n), lambda i: (0, 0)),
            pl.BlockSpec((1, n), lambda i: (0, 0)),
        ],
        out_specs=pl.BlockSpec((tm, n), lambda i: (i, 0)),
        out_shape=jax.ShapeDtypeStruct((s, n), F32),
        scratch_shapes=[pltpu.VMEM((1, n), F32)],
        compiler_params=_params("arbitrary"),
        name="forget_gate_cumsum",
    )(xb, ss, w_f, b_f)


GATE_PIECES = 3
V_ROWS = HEAD_DIM + 16
QUERY_TILE = 512


def _transposed_bf16(t):
    return t.astype(F32).T.astype(BF16)


def _attn_kernel(q_ref, qc_ref, k_ref, kc_ref, v_ref, o_ref, m_ref, acc_ref, qt_ref, vt_ref, *, tq, tk):
    i = pl.program_id(1)
    seq = v_ref.shape[0]

    @pl.when(i == 0)
    def _():
        def put(cidx, carry):
            at = pl.multiple_of(cidx * tk, tk)
            vt_ref[:HEAD_DIM, pl.ds(at, tk)] = _transposed_bf16(v_ref[pl.ds(at, tk), :])
            return carry

        lax.fori_loop(0, seq // tk, put, 0)
        extra = lax.broadcasted_iota(jnp.int32, (V_ROWS - HEAD_DIM, seq), 0)
        vt_ref[HEAD_DIM:, :] = jnp.where(extra == 0, 1.0, 0.0).astype(BF16)

    qt_ref[:HEAD_DIM, :] = _transposed_bf16(q_ref[...])
    qt_ref[HEAD_DIM:, :] = _transposed_bf16(qc_ref[0])
    m_ref[...] = jnp.full(m_ref.shape, -jnp.inf, F32)
    acc_ref[...] = jnp.zeros(acc_ref.shape, F32)

    def keys(start, rows):
        return jnp.concatenate([k_ref[pl.ds(start, rows), :], kc_ref[0, pl.ds(start, rows), :]], axis=1)

    def scores(n, k_rows):
        cols = slice(n * QUERY_TILE, (n + 1) * QUERY_TILE)
        return jnp.dot(k_rows, qt_ref[:, cols], preferred_element_type=F32)

    def attend(n, st, vt_cols, diag_off):
        cols = slice(n * QUERY_TILE, (n + 1) * QUERY_TILE)
        if diag_off is not None:
            key = lax.broadcasted_iota(jnp.int32, st.shape, 0)
            qry = lax.broadcasted_iota(jnp.int32, st.shape, 1) + diag_off
            st = jnp.where(key <= qry, st, -jnp.inf)
        m_old = m_ref[:, cols]
        m_new = jnp.maximum(m_old, jnp.max(st, axis=0, keepdims=True))
        pt = jnp.exp2(st - m_new).astype(BF16)
        alpha = jnp.exp2(m_old - m_new)
        acc_ref[:, cols] = alpha * acc_ref[:, cols] + jnp.dot(vt_cols, pt, preferred_element_type=F32)
        m_ref[:, cols] = m_new

    def run(units):
        loaded = {}

        def block_keys(start):
            if id(start) not in loaded:
                loaded[id(start)] = keys(start, tk)
            return loaded[id(start)]

        st = scores(units[0][0], block_keys(units[0][1]))
        st_next = None
        for u, (n, start, on_diag) in enumerate(units):
            if u + 1 < len(units):
                st_next = scores(units[u + 1][0], block_keys(units[u + 1][1]))
            attend(n, st, vt_ref[:, pl.ds(start, tk)], 0 if on_diag else None)
            st = st_next

    def body(jj, carry):
        starts = [pl.multiple_of((jj * blocks_per_trip + b) * tk, tk) for b in range(blocks_per_trip)]
        run([(n, start, False) for start in starts for n in range(nt)])
        return carry

    nt = tq // QUERY_TILE
    blocks_per_trip = tq // tk
    lax.fori_loop(0, i, body, 0)
    base = pl.multiple_of(i * tq, tq)
    diag_starts = [base + kb * tk for kb in range(nt)]
    run([(n, diag_starts[kb], kb == n) for kb in range(nt) for n in range(kb, nt)])
    inv_l = 1.0 / acc_ref[HEAD_DIM:HEAD_DIM + 1, :]
    o_ref[...] = (acc_ref[:HEAD_DIM, :] * inv_l).T.astype(BF16)


def _split_bf16(t):
    pieces = []
    for _ in range(GATE_PIECES):
        top = lax.bitcast_convert_type(
            lax.bitcast_convert_type(t, jnp.uint32) & jnp.uint32(0xFFFF0000), F32)
        pieces.append(top.astype(BF16))
        t = t - top
    return pieces


def _gate_columns(c, gate_first):
    cp = jnp.stack(_split_bf16(c * LOG2E), axis=-1)
    const = jnp.full(cp.shape, 1.0 if gate_first else -1.0, BF16)
    cols = jnp.concatenate([cp, const] if gate_first else [const, cp], axis=-1)
    cols = jnp.pad(cols, ((0, 0), (0, 0), (0, HEAD_DIM - 2 * GATE_PIECES)))
    return cols.transpose(1, 0, 2)


def _attention(q, kv, c):
    s, d = q.shape
    nh = d // HEAD_DIM
    tq = _tile(s, 2048)
    tk = QUERY_TILE
    assert tq % QUERY_TILE == 0
    return pl.pallas_call(
        functools.partial(_attn_kernel, tq=tq, tk=tk),
        grid=(nh, s // tq),
        in_specs=[
            pl.BlockSpec((tq, HEAD_DIM), lambda h, i: (i, h)),
            pl.BlockSpec((1, tq, HEAD_DIM), lambda h, i: (h, i, 0)),
            pl.BlockSpec((s, HEAD_DIM), lambda h, i: (0, h)),
            pl.BlockSpec((1, s, HEAD_DIM), lambda h, i: (h, 0, 0)),
            pl.BlockSpec((s, HEAD_DIM), lambda h, i: (0, nh + h)),
        ],
        out_specs=pl.BlockSpec((tq, HEAD_DIM), lambda h, i: (i, h)),
        out_shape=jax.ShapeDtypeStruct((s, d), BF16),
        scratch_shapes=[
            pltpu.VMEM((1, tq), F32),
            pltpu.VMEM((V_ROWS, tq), F32),
            pltpu.VMEM((2 * HEAD_DIM, tq), BF16),
            pltpu.VMEM((V_ROWS, s), BF16),
        ],
        compiler_params=_params("arbitrary", "arbitrary"),
        name="forgetting_attention",
    )(q, _gate_columns(c, False), kv, _gate_columns(c, True), kv)


def _gained(g, w):
    return (g[..., :, None] * w).astype(BF16)


def kernel(x, p, pool_norm, w_pool, pool_scale, attn_norm, w_q, q_norm, w_o, kv_norm, w_kvf, b_f,
           k_norm, ffn_norm, w_in, conv_w, conv_b, w_out, w_ple, gate_norm, w_gate):
    b, s, d = x.shape
    assert b == 1 and d % (HEAD_DIM * len(POOL_WINDOWS)) == 0
    depth = p.shape[0]
    n_a = pool_norm.shape[0]
    nh = d // HEAD_DIM
    assert nh <= LANE
    x = x[0]
    row = lambda t: t[None, :]
    f = w_out.shape[1]
    halves = lambda t: (t[..., :f], t[..., f:])
    w_in_halves = [_gained(ffn_norm, t) for t in halves(w_in)]
    ffn_in_consts = (*w_in_halves, *halves(conv_w), *halves(conv_b[:, None, :]))
    w_out_b = w_out.astype(BF16)
    w_gate_b = _gained(gate_norm, w_gate)
    w_ple_b = w_ple.astype(BF16)
    w_o_b = w_o.astype(BF16)
    w_q_b = _gained(attn_norm, w_q)
    kv = c_heads = xb = ss = None
    for i in range(depth):
        if i < n_a:
            x, xb, ss = _pool_layer(x, row(pool_norm[i]), w_pool[i].astype(BF16), row(pool_scale[i]))
        else:
            j = i - n_a
            q = _proj_heads(xb, ss, w_q_b[j], row(q_norm[j]), d, d, HEAD_DIM ** -0.5 * LOG2E)
            att = _attention(q, kv, c_heads)
            x, xb, ss = _mm_res(att, w_o_b, j, x, 1024, 512, d)
        act = _ffn_in(xb, ss, *ffn_in_consts, i)
        x, xb, ss = _mm_res(act, w_out_b, i, x, 512, 1024, f // 2 if f % (2 * LANE) == 0 else f)
        x, xb, ss = _ple(xb, ss, w_gate_b, p, w_ple_b, i, x)
        if i == n_a - 1:
            w_kvf_g = _gained(kv_norm, w_kvf)
            kv = _proj_heads(xb, ss, w_kvf_g, row(k_norm), 2 * d, d, 1.0)
            w_f = jnp.pad(w_kvf_g[:, 2 * d:], ((0, 0), (0, LANE - nh)))
            c = _forget_cumsum(xb, ss, w_f, jnp.pad(b_f, (0, LANE - nh))[None, :])
            c_heads = c[:, :nh]
    return x[None]
```

```python
import functools
import math

import jax
import jax.numpy as jnp
from jax import lax
from jax.experimental import pallas as pl
from jax.experimental.pallas import tpu as pltpu

HEAD_DIM = 128
POOL_WINDOWS = (2, 4, 8, 16)
POOL_HALO = 16
CONV_WIDTH = 3
CARRY_ROWS = 8
EPS = 1e-6
LOG2E = math.log2(math.e)
LANE = 128
VMEM_LIMIT = 56 * 1024 * 1024

F32 = jnp.float32
BF16 = jnp.bfloat16


def _tile(n, want):
    t = min(n, want)
    while n % t:
        t //= 2
    return t


def _params(*sem):
    return pltpu.CompilerParams(dimension_semantics=sem, vmem_limit_bytes=VMEM_LIMIT)


def _rms(x, g):
    ms = jnp.mean(x * x, axis=-1, keepdims=True)
    return x * lax.rsqrt(ms + EPS) * g


def _rstd(ss_ref, d):
    return lax.rsqrt(ss_ref[:, :1] * (1.0 / d) + EPS)


def _emit_stream(out, j, o_ref, ob_ref, ss_ref):
    o_ref[...] = out
    ob_ref[...] = out.astype(BF16)
    part = jnp.broadcast_to(jnp.sum(out * out, axis=1, keepdims=True), ss_ref.shape)

    @pl.when(j == 0)
    def _():
        ss_ref[...] = part

    @pl.when(j > 0)
    def _():
        ss_ref[...] += part


def _stream_shapes(s, d):
    return [jax.ShapeDtypeStruct((s, d), F32), jax.ShapeDtypeStruct((s, d), BF16),
            jax.ShapeDtypeStruct((s, LANE), F32)]


def _pool_kernel(x_ref, halo_ref, g_ref, w_ref, sc_ref, xo_ref, xb_ref, ss_ref, *, tm):
    i = pl.program_id(0)
    x = x_ref[...]
    g = g_ref[...]
    h = _rms(x, g)
    hh = jnp.where(i > 0, _rms(halo_ref[...], g), 0.0)
    cur = jnp.concatenate([hh, h], axis=0)
    d = x.shape[1]
    c = d // len(POOL_WINDOWS)
    pos = i * tm + lax.broadcasted_iota(jnp.int32, (tm, 1), 0)
    outs = []
    for gi, w in enumerate(POOL_WINDOWS):
        cur = cur + pltpu.roll(cur, w // 2, axis=0)
        win = cur[POOL_HALO:, :c]
        if gi + 1 < len(POOL_WINDOWS):
            cur = cur[:, c:]
        inv_cnt = 1.0 / jnp.minimum(pos + 1, w).astype(F32)
        pooled = win * inv_cnt - h[:, gi * c:(gi + 1) * c]
        outs.append(jnp.dot(pooled.astype(BF16), w_ref[gi], preferred_element_type=F32))
    x1 = x + jnp.concatenate(outs, axis=1) * sc_ref[...]
    xo_ref[...] = x1
    xb_ref[...] = x1.astype(BF16)
    ss_ref[...] = jnp.broadcast_to(jnp.sum(x1 * x1, axis=1, keepdims=True), ss_ref.shape)


def _pool_layer(x, g, w_pool, scale):
    s, d = x.shape
    tm = _tile(s, 256)
    hb = tm // POOL_HALO
    row = lambda i: (i, 0)
    const2 = lambda i: (0, 0)
    return pl.pallas_call(
        functools.partial(_pool_kernel, tm=tm),
        grid=(s // tm,),
        in_specs=[
            pl.BlockSpec((tm, d), row),
            pl.BlockSpec((POOL_HALO, d), lambda i: (jnp.maximum(i * hb - 1, 0), 0)),
            pl.BlockSpec((1, d), const2),
            pl.BlockSpec(w_pool.shape, lambda i: (0, 0, 0)),
            pl.BlockSpec((1, d), const2),
        ],
        out_specs=[pl.BlockSpec((tm, d), row), pl.BlockSpec((tm, d), row), pl.BlockSpec((tm, LANE), row)],
        out_shape=_stream_shapes(s, d),
        compiler_params=_params("arbitrary"),
        name="pool_mixer",
    )(x, x, g, w_pool, scale)


def _causal_conv(u, prev, cw, b):
    tm = u.shape[0]
    ext = jnp.concatenate([prev, u], axis=0)
    u1 = pltpu.roll(ext, 1, axis=0)[CARRY_ROWS:]
    u2 = pltpu.roll(ext, 2, axis=0)[CARRY_ROWS:]
    return cw[2:3] * u + cw[1:2] * u1 + cw[0:1] * u2 + b, ext[tm:]


def _ffn_in_kernel(a_ref, ss_ref, wg_ref, wv_ref, cwg_ref, cwv_ref, bg_ref, bv_ref, o_ref, carry_ref):
    i = pl.program_id(0)
    j = pl.program_id(1)

    @pl.when(i == 0)
    def _():
        carry_ref[j] = jnp.zeros(carry_ref.shape[1:], F32)

    a = a_ref[...]
    rstd = _rstd(ss_ref, a.shape[1])
    ug = jnp.dot(a, wg_ref[...], preferred_element_type=F32) * rstd
    uv = jnp.dot(a, wv_ref[...], preferred_element_type=F32) * rstd
    yg, tail_g = _causal_conv(ug, carry_ref[j, 0], cwg_ref[...], bg_ref[...])
    yv, tail_v = _causal_conv(uv, carry_ref[j, 1], cwv_ref[...], bv_ref[...])
    carry_ref[j, 0] = tail_g
    carry_ref[j, 1] = tail_v
    o_ref[...] = (yg * jax.nn.sigmoid(yg) * yv).astype(BF16)


def _ffn_in(xb, ss, w_gate, w_val, cw_gate, cw_val, b_gate, b_val, layer):
    s, d = xb.shape
    f = w_gate.shape[2]
    tm = _tile(s, 1024)
    tn = min(f, 512)
    assert tn % LANE == 0
    nj = pl.cdiv(f, tn)
    col = lambda i, j: (layer, 0, j)
    return pl.pallas_call(
        _ffn_in_kernel,
        grid=(s // tm, nj),
        in_specs=[
            pl.BlockSpec((tm, d), lambda i, j: (i, 0)),
            pl.BlockSpec((tm, LANE), lambda i, j: (i, 0)),
            pl.BlockSpec((None, d, tn), col),
            pl.BlockSpec((None, d, tn), col),
            pl.BlockSpec((None, CONV_WIDTH, tn), col),
            pl.BlockSpec((None, CONV_WIDTH, tn), col),
            pl.BlockSpec((None, 1, tn), col),
            pl.BlockSpec((None, 1, tn), col),
        ],
        out_specs=pl.BlockSpec((tm, tn), lambda i, j: (i, j)),
        out_shape=jax.ShapeDtypeStruct((s, f), BF16),
        scratch_shapes=[pltpu.VMEM((nj, 2, CARRY_ROWS, tn), F32)],
        compiler_params=_params("arbitrary", "arbitrary"),
        name="ffn_in_conv_act",
    )(xb, ss, w_gate, w_val, cw_gate, cw_val, b_gate, b_val)


def _mm_res_kernel(a_ref, w_ref, r_ref, o_ref, ob_ref, ss_ref, *acc, nk, tn):
    k = pl.program_id(1)
    j = pl.program_id(2)
    prod = jnp.dot(a_ref[...], w_ref[...], preferred_element_type=F32)
    if nk == 1:
        _emit_stream(r_ref[...] + prod, j, o_ref, ob_ref, ss_ref)
        return
    (acc_ref,) = acc
    cols = pl.ds(pl.multiple_of(j * tn, tn), tn)

    @pl.when(k == 0)
    def _():
        acc_ref[:, cols] = prod

    @pl.when(jnp.logical_and(k > 0, k < nk - 1))
    def _():
        acc_ref[:, cols] += prod

    @pl.when(k == nk - 1)
    def _():
        _emit_stream(r_ref[...] + (acc_ref[:, cols] + prod), j, o_ref, ob_ref, ss_ref)


def _mm_res(a, w, layer, res, tm_want, tn_want, tk_want):
    s, kdim = a.shape
    n = w.shape[2]
    tm = _tile(s, tm_want)
    tn = _tile(n, tn_want)
    tk = _tile(kdim, tk_want)
    assert tk % LANE == 0
    nk = kdim // tk
    tile = pl.BlockSpec((tm, tn), lambda i, k, j: (i, jnp.where(k == nk - 1, j, 0)))
    return pl.pallas_call(
        functools.partial(_mm_res_kernel, nk=nk, tn=tn),
        grid=(s // tm, nk, n // tn),
        in_specs=[
            pl.BlockSpec((tm, tk), lambda i, k, j: (i, k)),
            pl.BlockSpec((None, tk, tn), lambda i, k, j: (layer, k, j)),
            tile,
        ],
        out_specs=[tile, tile, pl.BlockSpec((tm, LANE), lambda i, k, j: (i, 0))],
        out_shape=_stream_shapes(s, n),
        scratch_shapes=[pltpu.VMEM((tm, n), F32)] if nk > 1 else [],
        compiler_params=_params("arbitrary", "arbitrary", "arbitrary"),
        name="matmul_residual",
    )(a, w, res)


def _ple_kernel(a_ref, ss_ref, wg_ref, p_ref, wp_ref, x_ref, o_ref, ob_ref, sso_ref):
    a = a_ref[...]
    z = jnp.dot(a, wg_ref[...], preferred_element_type=F32) * _rstd(ss_ref, a.shape[1])
    e = jnp.dot(p_ref[...].astype(BF16), wp_ref[...], preferred_element_type=F32)
    _emit_stream(x_ref[...] + jax.nn.sigmoid(z) * e, pl.program_id(1), o_ref, ob_ref, sso_ref)


def _ple(xb, ss, w_gate, p, w_ple, layer, x):
    s, d = x.shape
    pd = p.shape[3]
    tm = _tile(s, 1024)
    tn = _tile(d, 512)
    tile = pl.BlockSpec((tm, tn), lambda i, j: (i, j))
    stats = pl.BlockSpec((tm, LANE), lambda i, j: (i, 0))
    return pl.pallas_call(
        _ple_kernel,
        grid=(s // tm, d // tn),
        in_specs=[
            pl.BlockSpec((tm, d), lambda i, j: (i, 0)),
            stats,
            pl.BlockSpec((None, d, tn), lambda i, j: (layer, 0, j)),
            pl.BlockSpec((None, None, tm, pd), lambda i, j: (layer, 0, i, 0)),
            pl.BlockSpec((None, pd, tn), lambda i, j: (layer, 0, j)),
            tile,
        ],
        out_specs=[tile, tile, stats],
        out_shape=_stream_shapes(s, d),
        compiler_params=_params("arbitrary", "arbitrary"),
        name="per_layer_embed",
    )(xb, ss, w_gate, p, w_ple, x)


def _proj_kernel(a_ref, ss_ref, w_ref, g_ref, o_ref, *, n_norm_tiles, out_scale):
    j = pl.program_id(1)
    a = a_ref[...]
    z = jnp.dot(a, w_ref[...], preferred_element_type=F32) * _rstd(ss_ref, a.shape[1])
    tn = z.shape[1]

    @pl.when(j < n_norm_tiles)
    def _():
        g = g_ref[...] * out_scale
        for hh in range(tn // HEAD_DIM):
            sl = slice(hh * HEAD_DIM, (hh + 1) * HEAD_DIM)
            zh = z[:, sl]
            ms = jnp.mean(zh * zh, axis=-1, keepdims=True)
            o_ref[:, sl] = (zh * lax.rsqrt(ms + EPS) * g).astype(BF16)

    @pl.when(j >= n_norm_tiles)
    def _():
        o_ref[...] = z.astype(BF16)


def _proj_heads(xb, ss, w, head_gain, n, n_norm_cols, out_scale):
    s, d = xb.shape
    tm = _tile(s, 1024)
    tn = _tile(math.gcd(n, n_norm_cols), 1024)
    assert tn % HEAD_DIM == 0
    return pl.pallas_call(
        functools.partial(_proj_kernel, n_norm_tiles=n_norm_cols // tn, out_scale=out_scale),
        grid=(s // tm, n // tn),
        in_specs=[
            pl.BlockSpec((tm, d), lambda i, j: (i, 0)),
            pl.BlockSpec((tm, LANE), lambda i, j: (i, 0)),
            pl.BlockSpec((d, tn), lambda i, j: (0, j)),
            pl.BlockSpec((1, HEAD_DIM), lambda i, j: (0, 0)),
        ],
        out_specs=pl.BlockSpec((tm, tn), lambda i, j: (i, j)),
        out_shape=jax.ShapeDtypeStruct((s, n), BF16),
        compiler_params=_params("arbitrary", "arbitrary"),
        name="head_projection",
    )(xb, ss, w, head_gain)


def _gate_kernel(a_ref, ss_ref, w_ref, b_ref, o_ref, carry_ref, *, tm):
    i = pl.program_id(0)

    @pl.when(i == 0)
    def _():
        carry_ref[...] = jnp.zeros(carry_ref.shape, F32)

    a = a_ref[...]
    z = jnp.dot(a, w_ref[...], preferred_element_type=F32) * _rstd(ss_ref, a.shape[1]) + b_ref[...]
    run = jnp.minimum(z, 0.0) - jnp.log1p(jnp.exp(-jnp.abs(z)))
    row = lax.broadcasted_iota(jnp.int32, run.shape, 0)
    shift = 1
    while shift < tm:
        run = run + jnp.where(row >= shift, pltpu.roll(run, shift, axis=0), 0.0)
        shift *= 2
    run = run + carry_ref[...]
    o_ref[...] = run
    carry_ref[...] = run[tm - 1:tm, :]


def _forget_cumsum(xb, ss, w_f, b_f):
    s, d = xb.shape
    n = w_f.shape[1]
    tm = _tile(s, 512)
    return pl.pallas_call(
        functools.partial(_gate_kernel, tm=tm),
        grid=(s // tm,),
        in_specs=[
            pl.BlockSpec((tm, d), lambda i: (i, 0)),
            pl.BlockSpec((tm, LANE), lambda i: (i, 0)),
            pl.BlockSpec((d, n), lambda i: (0, 0)),
            pl.BlockSpec((1, n), lambda i: (0, 0)),
        ],
        out_specs=pl.BlockSpec((tm, n), lambda i: (i, 0)),
        out_shape=jax.ShapeDtypeStruct((s, n), F32),
        scratch_shapes=[pltpu.VMEM((1, n), F32)],
        compiler_params=_params("arbitrary"),
        name="forget_gate_cumsum",
    )(xb, ss, w_f, b_f)


GATE_PIECES = 3
V_ROWS = HEAD_DIM + 16
QUERY_TILE = 512


def _transposed_bf16(t):
    return t.astype(F32).T.astype(BF16)


def _attn_kernel(q_ref, qc_ref, k_ref, kc_ref, v_ref, o_ref, m_ref, acc_ref, qt_ref, vt_ref, *, tq, tk):
    i = pl.program_id(1)
    seq = v_ref.shape[0]

    @pl.when(i == 0)
    def _():
        def put(cidx, carry):
            at = pl.multiple_of(cidx * tk, tk)
            vt_ref[:HEAD_DIM, pl.ds(at, tk)] = _transposed_bf16(v_ref[pl.ds(at, tk), :])
            return carry

        lax.fori_loop(0, seq // tk, put, 0)
        extra = lax.broadcasted_iota(jnp.int32, (V_ROWS - HEAD_DIM, seq), 0)
        vt_ref[HEAD_DIM:, :] = jnp.where(extra == 0, 1.0, 0.0).astype(BF16)

    qt_ref[:HEAD_DIM, :] = _transposed_bf16(q_ref[...])
    qt_ref[HEAD_DIM:, :] = _transposed_bf16(qc_ref[0])
    m_ref[...] = jnp.full(m_ref.shape, -jnp.inf, F32)
    acc_ref[...] = jnp.zeros(acc_ref.shape, F32)

    def keys(start, rows):
        return jnp.concatenate([k_ref[pl.ds(start, rows), :], kc_ref[0, pl.ds(start, rows), :]], axis=1)

    def scores(n, k_rows):
        cols = slice(n * QUERY_TILE, (n + 1) * QUERY_TILE)
        return jnp.dot(k_rows, qt_ref[:, cols], preferred_element_type=F32)

    def attend(n, st, vt_cols, diag_off):
        cols = slice(n * QUERY_TILE, (n + 1) * QUERY_TILE)
        if diag_off is not None:
            key = lax.broadcasted_iota(jnp.int32, st.shape, 0)
            qry = lax.broadcasted_iota(jnp.int32, st.shape, 1) + diag_off
            st = jnp.where(key <= qry, st, -jnp.inf)
        m_old = m_ref[:, cols]
        m_new = jnp.maximum(m_old, jnp.max(st, axis=0, keepdims=True))
        pt = jnp.exp2(st - m_new).astype(BF16)
        alpha = jnp.exp2(m_old - m_new)
        acc_ref[:, cols] = alpha * acc_ref[:, cols] + jnp.dot(vt_cols, pt, preferred_element_type=F32)
        m_ref[:, cols] = m_new

    def run(units):
        loaded = {}

        def block_keys(start):
            if id(start) not in loaded:
                loaded[id(start)] = keys(start, tk)
            return loaded[id(start)]

        st = scores(units[0][0], block_keys(units[0][1]))
        st_next = None
        for u, (n, start, on_diag) in enumerate(units):
            if u + 1 < len(units):
                st_next = scores(units[u + 1][0], block_keys(units[u + 1][1]))
            attend(n, st, vt_ref[:, pl.ds(start, tk)], 0 if on_diag else None)
            st = st_next

    def body(jj, carry):
        starts = [pl.multiple_of((jj * blocks_per_trip + b) * tk, tk) for b in range(blocks_per_trip)]
        run([(n, start, False) for start in starts for n in range(nt)])
        return carry

    nt = tq // QUERY_TILE
    blocks_per_trip = tq // tk
    lax.fori_loop(0, i, body, 0)
    base = pl.multiple_of(i * tq, tq)
    diag_starts = [base + kb * tk for kb in range(nt)]
    run([(n, diag_starts[kb], kb == n) for kb in range(nt) for n in range(kb, nt)])
    inv_l = 1.0 / acc_ref[HEAD_DIM:HEAD_DIM + 1, :]
    o_ref[...] = (acc_ref[:HEAD_DIM, :] * inv_l).T.astype(BF16)


def _split_bf16(t):
    pieces = []
    for _ in range(GATE_PIECES):
        top = lax.bitcast_convert_type(
            lax.bitcast_convert_type(t, jnp.uint32) & jnp.uint32(0xFFFF0000), F32)
        pieces.append(top.astype(BF16))
        t = t - top
    return pieces


def _gate_columns(c, gate_first):
    cp = jnp.stack(_split_bf16(c.T * LOG2E), axis=-1)
    const = jnp.full(cp.shape, 1.0 if gate_first else -1.0, BF16)
    cols = jnp.concatenate([cp, const] if gate_first else [const, cp], axis=-1)
    return jnp.pad(cols, ((0, 0), (0, 0), (0, HEAD_DIM - 2 * GATE_PIECES)))


def _attention(q, kv, c):
    s, d = q.shape
    nh = d // HEAD_DIM
    tq = _tile(s, 2048)
    tk = QUERY_TILE
    assert tq % QUERY_TILE == 0
    return pl.pallas_call(
        functools.partial(_attn_kernel, tq=tq, tk=tk),
        grid=(nh, s // tq),
        in_specs=[
            pl.BlockSpec((tq, HEAD_DIM), lambda h, i: (i, h)),
            pl.BlockSpec((1, tq, HEAD_DIM), lambda h, i: (h, i, 0)),
            pl.BlockSpec((s, HEAD_DIM), lambda h, i: (0, h)),
            pl.BlockSpec((1, s, HEAD_DIM), lambda h, i: (h, 0, 0)),
            pl.BlockSpec((s, HEAD_DIM), lambda h, i: (0, nh + h)),
        ],
        out_specs=pl.BlockSpec((tq, HEAD_DIM), lambda h, i: (i, h)),
        out_shape=jax.ShapeDtypeStruct((s, d), BF16),
        scratch_shapes=[
            pltpu.VMEM((1, tq), F32),
            pltpu.VMEM((V_ROWS, tq), F32),
            pltpu.VMEM((2 * HEAD_DIM, tq), BF16),
            pltpu.VMEM((V_ROWS, s), BF16),
        ],
        compiler_params=_params("arbitrary", "arbitrary"),
        name="forgetting_attention",
    )(q, _gate_columns(c, False), kv, _gate_columns(c, True), kv)


def _gained(g, w):
    return (g[..., :, None] * w).astype(BF16)


def kernel(x, p, pool_norm, w_pool, pool_scale, attn_norm, w_q, q_norm, w_o, kv_norm, w_kvf, b_f,
           k_norm, ffn_norm, w_in, conv_w, conv_b, w_out, w_ple, gate_norm, w_gate):
    b, s, d = x.shape
    assert b == 1 and d % (HEAD_DIM * len(POOL_WINDOWS)) == 0
    depth = p.shape[0]
    n_a = pool_norm.shape[0]
    nh = d // HEAD_DIM
    assert nh <= LANE
    x = x[0]
    row = lambda t: t[None, :]
    f = w_out.shape[1]
    halves = lambda t: (t[..., :f], t[..., f:])
    w_in_halves = [_gained(ffn_norm, t) for t in halves(w_in)]
    ffn_in_consts = (*w_in_halves, *halves(conv_w), *halves(conv_b[:, None, :]))
    w_out_b = w_out.astype(BF16)
    w_gate_b = _gained(gate_norm, w_gate)
    w_ple_b = w_ple.astype(BF16)
    w_o_b = w_o.astype(BF16)
    w_q_b = _gained(attn_norm, w_q)
    kv = c_heads = xb = ss = None
    for i in range(depth):
        if i < n_a:
            x, xb, ss = _pool_layer(x, row(pool_norm[i]), w_pool[i].astype(BF16), row(pool_scale[i]))
        else:
            j = i - n_a
            q = _proj_heads(xb, ss, w_q_b[j], row(q_norm[j]), d, d, HEAD_DIM ** -0.5 * LOG2E)
            att = _attention(q, kv, c_heads)
            x, xb, ss = _mm_res(att, w_o_b, j, x, 1024, 512, d)
        act = _ffn_in(xb, ss, *ffn_in_consts, i)
        x, xb, ss = _mm_res(act, w_out_b, i, x, 512, 1024, f // 2 if f % (2 * LANE) == 0 else f)
        x, xb, ss = _ple(xb, ss, w_gate_b, p, w_ple_b, i, x)
        if i == n_a - 1:
            w_kvf_g = _gained(kv_norm, w_kvf)
            kv = _proj_heads(xb, ss, w_kvf_g, row(k_norm), 2 * d, d, 1.0)
            w_f = jnp.pad(w_kvf_g[:, 2 * d:], ((0, 0), (0, LANE - nh)))
            c = _forget_cumsum(xb, ss, w_f, jnp.pad(b_f, (0, LANE - nh))[None, :])
            c_heads = c[:, :nh]
    return x[None]
```
